```python
import math
import jax, jax.numpy as jnp
from jax import lax
import numpy as np

D_MODEL = 1024
BATCH = 4
SEQ = 4096
DEPTH = 1
DEC_BATCH = 128
DEC_SEQ = 8
PAST_LEN = 8192
PAGE_SIZE = 128

D_MIX = D_MODEL
D_RET = D_MIX // 2
D_DIFF = D_MIX - D_RET
H_RET = 4
DK_RET = D_RET // H_RET
DV_RET = D_RET // H_RET
H_DIFF = 4
DV_DIFF = D_DIFF // H_DIFF
DK_DIFF = DV_DIFF // 2
D_IN = 4 * D_RET + 4 * D_DIFF
RET_CHUNK = 128
Q_BLOCK = 128
ROPE_BASE = 10000.0
NORM_EPS = 1e-6
GN_EPS = 1e-5
LAMBDA_STD = 0.1

kernel_name = "hymba_retention_diffattn_step"

F32 = jnp.float32


def rmsnorm(x, g):
    x32 = x.astype(F32)
    y = x32 * lax.rsqrt(jnp.mean(x32 * x32, axis=-1, keepdims=True) + NORM_EPS)
    return (y * g.astype(F32)).astype(x.dtype)


def retention_log_gammas():
    return jnp.log(1.0 - jnp.exp2(-5.0 - jnp.arange(H_RET, dtype=F32)))


def alibi_slopes():
    return jnp.exp2(-8.0 * (jnp.arange(H_DIFF, dtype=F32) + 1.0) / H_DIFF)


def rope(x, pos):
    half = x.shape[-1] // 2
    inv = ROPE_BASE ** (-jnp.arange(half, dtype=F32) / half)
    ang = pos.astype(F32)[:, None] * inv[None, :]
    cos = jnp.cos(ang)[None, :, None, :]
    sin = jnp.sin(ang)[None, :, None, :]
    x = x.astype(F32)
    x1, x2 = x[..., :half], x[..., half:]
    return jnp.concatenate([x1 * cos - x2 * sin, x1 * sin + x2 * cos], axis=-1)


def project(x, pos, norm_pre, w_in):
    B, L = x.shape[0], x.shape[1]
    h = rmsnorm(x, norm_pre)
    z = jnp.einsum('bld,de->ble', h, w_in)
    cuts = list(np.cumsum([D_RET] * 4 + [D_DIFF] * 3))
    qr, kr, vr, gr, qd, kd, vd, gd = jnp.split(z, cuts, axis=-1)
    qr = rope(qr.reshape(B, L, H_RET, DK_RET), pos)
    kr = rope(kr.reshape(B, L, H_RET, DK_RET), pos) * (DK_RET ** -0.5)
    vr = vr.reshape(B, L, H_RET, DV_RET)
    qd = qd.reshape(B, L, H_DIFF, 2 * DK_DIFF)
    kd = kd.reshape(B, L, H_DIFF, 2 * DK_DIFF)
    vd = vd.reshape(B, L, H_DIFF, DV_DIFF)
    return qr, kr, vr, gr, qd, kd, vd, gd


def retention(q, k, v, state0, chunk):
    B, L, H, Dk = q.shape
    Dv = v.shape[-1]
    n = L // chunk
    log_g = retention_log_gammas()
    idx = jnp.arange(chunk, dtype=F32)
    dist = idx[:, None] - idx[None, :]
    dmask = jnp.where(dist[None] >= 0, jnp.exp(jnp.maximum(dist, 0.0)[None] * log_g[:, None, None]), 0.0)
    xi = jnp.exp((idx[:, None] + 1.0) * log_g[None, :])
    zeta = jnp.exp((chunk - 1.0 - idx[:, None]) * log_g[None, :])
    g_chunk = jnp.exp(chunk * log_g)

    def to_chunks(t):
        return t.astype(F32).reshape(B, n, chunk, H, t.shape[-1]).transpose(1, 0, 2, 3, 4)

    def step(S, xs):
        qc, kc, vc = xs
        inner = jnp.einsum('bihd,bjhd->bhij', qc, kc) * dmask[None]
        o = jnp.einsum('bhij,bjhv->bihv', inner, vc)
        o = o + jnp.einsum('bihd,bhdv->bihv', qc, S) * xi[None, :, :, None]
        S_new = S * g_chunk[None, :, None, None] + jnp.einsum('bjhd,bjhv->bhdv', kc * zeta[None, :, :, None], vc)
        return S_new, o

    S_fin, o = lax.scan(step, state0.astype(F32), (to_chunks(q), to_chunks(k), to_chunks(v)))
    o = o.transpose(1, 0, 2, 3, 4).reshape(B, L, H, Dv)
    return o, S_fin


def ret_finish(o, gate, gn_gain):
    mu = jnp.mean(o, axis=-1, keepdims=True)
    var = jnp.mean(jnp.square(o - mu), axis=-1, keepdims=True)
    y = (o - mu) * lax.rsqrt(var + GN_EPS) * gn_gain.astype(F32)[None, None]
    B, L = o.shape[0], o.shape[1]
    return y.reshape(B, L, D_RET) * jax.nn.silu(gate.astype(F32))


def diff_prompt(qd, kd, vd, lam):
    B, L = qd.shape[0], qd.shape[1]
    scale = DK_DIFF ** -0.5
    slopes = alibi_slopes()
    k1 = kd[..., :DK_DIFF].astype(F32)
    k2 = kd[..., DK_DIFF:].astype(F32)
    v = vd.astype(F32)
    kpos = jnp.arange(L)

    def block(i):
        start = i * Q_BLOCK
        qb = lax.dynamic_slice_in_dim(qd, start, Q_BLOCK, axis=1).astype(F32)
        qpos = start + jnp.arange(Q_BLOCK)
        dist = (qpos[:, None] - kpos[None, :]).astype(F32)
        bias = jnp.where(dist[None] >= 0, -slopes[:, None, None] * dist[None], -jnp.inf)
        s1 = jnp.einsum('bqhd,bkhd->bhqk', qb[..., :DK_DIFF], k1) * scale + bias[None]
        s2 = jnp.einsum('bqhd,bkhd->bhqk', qb[..., DK_DIFF:], k2) * scale + bias[None]
        a = jax.nn.softmax(s1, axis=-1) - lam * jax.nn.softmax(s2, axis=-1)
        return jnp.einsum('bhqk,bkhv->bqhv', a, v)

    o = lax.map(block, jnp.arange(L // Q_BLOCK))
    return o.transpose(1, 0, 2, 3, 4).reshape(B, L, H_DIFF, DV_DIFF)


def online_update(m, l, acc, s, v):
    m_new = jnp.maximum(m, jnp.max(s, axis=-1))
    corr = jnp.exp(m - m_new)
    p = jnp.exp(s - m_new[..., None])
    return m_new, l * corr + jnp.sum(p, axis=-1), acc * corr[..., None] + jnp.einsum('bhqk,bkhv->bhqv', p, v)


def diff_sample(qd, kd, vd, cache_k, cache_v, page_table, lam):
    DB, T = qd.shape[0], qd.shape[1]
    n_pages = page_table.shape[1]
    past = n_pages * PAGE_SIZE
    scale = DK_DIFF ** -0.5
    slopes = alibi_slopes()
    q = qd.astype(F32)
    q1, q2 = q[..., :DK_DIFF], q[..., DK_DIFF:]
    qpos = past + jnp.arange(T)

    m0 = jnp.full((DB, H_DIFF, T), -jnp.inf, F32)
    l0 = jnp.zeros((DB, H_DIFF, T), F32)
    a0 = jnp.zeros((DB, H_DIFF, T, DV_DIFF), F32)
    kl = kd.astype(F32)
    vl = vd.astype(F32)
    dist = (qpos[:, None] - qpos[None, :]).astype(F32)
    bias = jnp.where(dist[None] >= 0, -slopes[:, None, None] * dist[None], -jnp.inf)[None]
    s1 = jnp.einsum('bqhd,bkhd->bhqk', q1, kl[..., :DK_DIFF]) * scale + bias
    s2 = jnp.einsum('bqhd,bkhd->bhqk', q2, kl[..., DK_DIFF:]) * scale + bias
    c1 = online_update(m0, l0, a0, s1, vl)
    c2 = online_update(m0, l0, a0, s2, vl)

    def step(carry, xs):
        (m1, l1, acc1), (m2, l2, acc2) = carry
        phys, j = xs
        kp = cache_k[phys].astype(F32)
        vp = cache_v[phys].astype(F32)
        kpos = j * PAGE_SIZE + jnp.arange(PAGE_SIZE)
        d = (qpos[:, None] - kpos[None, :]).astype(F32)
        b = (-slopes[:, None, None] * d[None])[None]
        p1 = jnp.einsum('bqhd,bkhd->bhqk', q1, kp[..., :DK_DIFF]) * scale + b
        p2 = jnp.einsum('bqhd,bkhd->bhqk', q2, kp[..., DK_DIFF:]) * scale + b
        return (online_update(m1, l1, acc1, p1, vp), online_update(m2, l2, acc2, p2, vp)), None

    ((m1, l1, acc1), (m2, l2, acc2)), _ = lax.scan(step, (c1, c2), (page_table.T, jnp.arange(n_pages)))
    o = acc1 / l1[..., None] - lam * (acc2 / l2[..., None])
    return o.transpose(0, 2, 1, 3)


def diff_finish(o, gate, gain, lam_init):
    y = o * lax.rsqrt(jnp.mean(o * o, axis=-1, keepdims=True) + NORM_EPS)
    y = y * gain.astype(F32)[None, None] * (1.0 - lam_init)
    B, L = o.shape[0], o.shape[1]
    return y.reshape(B, L, D_DIFF) * jax.nn.silu(gate.astype(F32))


def finish(x, o_ret, o_diff, w_out, norm_post):
    o = jnp.concatenate([o_ret, o_diff], axis=-1).astype(x.dtype)
    y = jnp.einsum('ble,ed->bld', o, w_out)
    return x + rmsnorm(y, norm_post)


def setup_inputs(seed: int = 0) -> dict:
    key = jax.random.key(seed)
    ks = jax.random.split(key, 18)
    n_pages = PAST_LEN // PAGE_SIZE
    n_used = DEC_BATCH * n_pages
    n_pool = (n_used * 5) // 4
    page_table = jax.random.permutation(ks[0], n_pool)[:n_used].reshape(DEC_BATCH, n_pages).astype(jnp.int32)
    return {
        "x_prompt": jax.random.normal(ks[1], (BATCH, SEQ, D_MODEL), F32),
        "x_sample": jax.random.normal(ks[2], (DEC_BATCH, DEC_SEQ, D_MODEL), F32),
        "cache_k": jax.random.normal(ks[3], (DEPTH, n_pool, PAGE_SIZE, H_DIFF, 2 * DK_DIFF), F32),
        "cache_v": jax.random.normal(ks[4], (DEPTH, n_pool, PAGE_SIZE, H_DIFF, DV_DIFF), F32),
        "page_table": page_table,
        "state_ret": 0.5 * jax.random.normal(ks[5], (DEPTH, DEC_BATCH, H_RET, DK_RET, DV_RET), F32),
        "norm_pre": 1.0 + 0.05 * jax.random.normal(ks[6], (DEPTH, D_MODEL), F32),
        "norm_post": 1.0 + 0.05 * jax.random.normal(ks[7], (DEPTH, D_MODEL), F32),
        "w_in": jax.random.normal(ks[8], (DEPTH, D_MODEL, D_IN), F32) * (D_MODEL ** -0.5),
        "ret_gn": 1.0 + 0.05 * jax.random.normal(ks[9], (DEPTH, H_RET, DV_RET), F32),
        "diff_lq1": LAMBDA_STD * jax.random.normal(ks[10], (DEPTH, DK_DIFF), F32),
        "diff_lk1": LAMBDA_STD * jax.random.normal(ks[11], (DEPTH, DK_DIFF), F32),
        "diff_lq2": LAMBDA_STD * jax.random.normal(ks[12], (DEPTH, DK_DIFF), F32),
        "diff_lk2": LAMBDA_STD * jax.random.normal(ks[13], (DEPTH, DK_DIFF), F32),
        "diff_norm": 1.0 + 0.05 * jax.random.normal(ks[14], (DEPTH, H_DIFF, DV_DIFF), F32),
        "w_out": jax.random.normal(ks[15], (DEPTH, D_MIX, D_MODEL), F32) * (D_MIX ** -0.5),
    }


def reference(x_prompt, x_sample, cache_k, cache_v, page_table, state_ret, norm_pre, norm_post, w_in,
              ret_gn, diff_lq1, diff_lk1, diff_lq2, diff_lk2, diff_norm, w_out):
    L_p = x_prompt.shape[1]
    T_s = x_sample.shape[1]
    past = page_table.shape[1] * PAGE_SIZE
    pos_p = jnp.arange(L_p)
    pos_s = past + jnp.arange(T_s)
    xp, xs = x_prompt, x_sample
    nk_p, nv_p, st_p, nk_s, nv_s, st_s = [], [], [], [], [], []
    for l in range(DEPTH):
        lam_init = 0.8 - 0.6 * math.exp(-0.3 * l)
        lam = (jnp.exp(jnp.sum(diff_lq1[l].astype(F32) * diff_lk1[l].astype(F32)))
               - jnp.exp(jnp.sum(diff_lq2[l].astype(F32) * diff_lk2[l].astype(F32))) + lam_init)
        qr, kr, vr, gr, qd, kd, vd, gd = project(xp, pos_p, norm_pre[l], w_in[l])
        s0 = jnp.zeros((xp.shape[0], H_RET, DK_RET, DV_RET), F32)
        o_r, s_fin = retention(qr, kr, vr, s0, RET_CHUNK)
        o_d = diff_prompt(qd, kd, vd, lam)
        xp = finish(xp, ret_finish(o_r, gr, ret_gn[l]), diff_finish(o_d, gd, diff_norm[l], lam_init), w_out[l], norm_post[l])
        nk_p.append(kd); nv_p.append(vd); st_p.append(s_fin)
        qr, kr, vr, gr, qd, kd, vd, gd = project(xs, pos_s, norm_pre[l], w_in[l])
        o_r, s_new = retention(qr, kr, vr, state_ret[l], T_s)
        o_d = diff_sample(qd, kd, vd, cache_k[l], cache_v[l], page_table, lam)
        xs = finish(xs, ret_finish(o_r, gr, ret_gn[l]), diff_finish(o_d, gd, diff_norm[l], lam_init), w_out[l], norm_post[l])
        nk_s.append(kd); nv_s.append(vd); st_s.append(s_new)
    return (xp, xs, jnp.stack(nk_p), jnp.stack(nv_p), jnp.stack(st_p), jnp.stack(nk_s), jnp.stack(nv_s), jnp.stack(st_s))
```

```python
import functools
import math

import jax
import jax.numpy as jnp
from jax import lax
from jax.experimental import pallas as pl
from jax.experimental.pallas import tpu as pltpu

F32 = jnp.float32
BF16 = jnp.bfloat16

H_RET = 4
H_DIFF = 4
D_HEAD = 128
DK_DIFF = 64
RET_CHUNK = 128
PAGE = 128
ROPE_BASE = 10000.0
NORM_EPS = 1e-6
GN_EPS = 1e-5
NEG_INF = float("-inf")

VMEM_LIMIT = 56 * 1024 * 1024


def _cparams(sem):
    return pltpu.CompilerParams(dimension_semantics=sem, vmem_limit_bytes=VMEM_LIMIT)


def _silu(x):
    return x * (1.0 / (1.0 + jnp.exp(-x)))


def _proj_kernel(x_ref, g_ref, w_ref, cos_ref, sin_ref,
                 qr_ref, kr_ref, vr_ref, gr_ref, qd_ref, kd_ref, vd_ref, gd_ref, kd16_ref, vd16_ref):
    d_grp = H_RET * D_HEAD
    x = x_ref[...]
    ms = jnp.mean(x * x, axis=-1, keepdims=True)
    h = (x * lax.rsqrt(ms + NORM_EPS) * g_ref[...]).astype(BF16)
    cos2 = cos_ref[...]
    sin2 = sin_ref[...]

    def section(s):
        return jnp.dot(h, w_ref[:, s * d_grp:(s + 1) * d_grp], preferred_element_type=F32)

    def rope(z, scale):
        for hd in range(H_RET):
            zh = z[:, hd * D_HEAD:(hd + 1) * D_HEAD]
            rot = pltpu.roll(zh, D_HEAD // 2, axis=1)
            yield hd, (zh * cos2 + rot * sin2) * scale

    z = section(0)
    for hd, r in rope(z, 1.0):
        qr_ref[:, hd * D_HEAD:(hd + 1) * D_HEAD] = r.astype(qr_ref.dtype)
    z = section(1)
    for hd, r in rope(z, D_HEAD ** -0.5):
        kr_ref[:, hd * D_HEAD:(hd + 1) * D_HEAD] = r.astype(kr_ref.dtype)
    vr_ref[...] = section(2).astype(vr_ref.dtype)
    gr_ref[...] = section(3)
    qd_ref[...] = (section(4) * (DK_DIFF ** -0.5)).astype(qd_ref.dtype)
    z = section(5)
    kd_ref[...] = z
    kd16_ref[...] = z.astype(BF16)
    z = section(6)
    vd_ref[...] = z
    vd16_ref[...] = z.astype(BF16)
    gd_ref[...] = section(7)


def _project(x, norm_pre, w_bf16, cos2, sin2, tm, ret_dtype):
    n, d_model = x.shape
    d_in = w_bf16.shape[1]
    d_grp = d_in // 8
    n_tab = cos2.shape[0] // tm
    tok = lambda i: (i, 0)
    out_blk = pl.BlockSpec((tm, d_grp), tok)
    shapes = [
        jax.ShapeDtypeStruct((n, d_grp), ret_dtype),
        jax.ShapeDtypeStruct((n, d_grp), ret_dtype),
        jax.ShapeDtypeStruct((n, d_grp), ret_dtype),
        jax.ShapeDtypeStruct((n, d_grp), F32),
        jax.ShapeDtypeStruct((n, d_grp), ret_dtype),
        jax.ShapeDtypeStruct((n, d_grp), F32),
        jax.ShapeDtypeStruct((n, d_grp), F32),
        jax.ShapeDtypeStruct((n, d_grp), F32),
        jax.ShapeDtypeStruct((n, d_grp), BF16),
        jax.ShapeDtypeStruct((n, d_grp), BF16),
    ]
    return pl.pallas_call(
        _proj_kernel,
        grid=(n // tm,),
        in_specs=[
            pl.BlockSpec((tm, d_model), tok),
            pl.BlockSpec((1, d_model), lambda i: (0, 0)),
            pl.BlockSpec((d_model, d_in), lambda i: (0, 0)),
            pl.BlockSpec((tm, D_HEAD), lambda i: (i % n_tab, 0)),
            pl.BlockSpec((tm, D_HEAD), lambda i: (i % n_tab, 0)),
        ],
        out_specs=[out_blk] * len(shapes),
        out_shape=shapes,
        compiler_params=_cparams(("parallel",)),
        name="proj",
    )(x, norm_pre.reshape(1, d_model), w_bf16, cos2, sin2)


def _rope_tables(pos):
    half = D_HEAD // 2
    inv = ROPE_BASE ** (-jnp.arange(half, dtype=F32) / half)
    ang = pos.astype(F32)[:, None] * inv[None, :]
    cos, sin = jnp.cos(ang), jnp.sin(ang)
    return jnp.concatenate([cos, cos], axis=-1), jnp.concatenate([-sin, sin], axis=-1)


def _group_norm_gate(o, gain, gate):
    mu = jnp.mean(o, axis=-1, keepdims=True)
    d = o - mu
    var = jnp.mean(d * d, axis=-1, keepdims=True)
    return d * lax.rsqrt(var + GN_EPS) * gain * _silu(gate)


def _decay_terms(n_rows, chunk, lg):
    row = lax.broadcasted_iota(jnp.int32, (n_rows, n_rows), 0)
    col = lax.broadcasted_iota(jnp.int32, (n_rows, n_rows), 1)
    dist = (row - col).astype(F32)
    dmask = jnp.where(col <= row, jnp.exp(jnp.maximum(dist, 0.0) * lg), 0.0)
    if n_rows != chunk:
        dmask = jnp.where(row // chunk == col // chunk, dmask, 0.0)
    idx = (lax.broadcasted_iota(jnp.int32, (n_rows, 1), 0) % chunk).astype(F32)
    xi = jnp.exp((idx + 1.0) * lg)
    zeta = jnp.exp((chunk - 1.0 - idx) * lg)
    g_chunk = jnp.exp(jnp.full((1, D_HEAD), chunk, F32) * lg)
    return dmask, xi, zeta, g_chunk


def _nt(a, b):
    return lax.dot_general(a, b, (((1,), (1,)), ((), ())), preferred_element_type=F32)


def _tn(a, b):
    return lax.dot_general(a, b, (((0,), (0,)), ((), ())), preferred_element_type=F32)


def _ret_prompt_kernel(lg_ref, q_ref, k_ref, v_ref, gate_ref, gn_ref, y_ref, s_out_ref, s_scr):
    c = pl.program_id(1)

    @pl.when(c == 0)
    def _():
        s_scr[...] = jnp.zeros_like(s_scr)

    for h in range(H_RET):
        sl = slice(h * D_HEAD, (h + 1) * D_HEAD)
        lg = lg_ref[h]
        dmask, xi, zeta, g_chunk = _decay_terms(RET_CHUNK, RET_CHUNK, lg)
        q = q_ref[:, sl]
        k = k_ref[:, sl]
        v = v_ref[:, sl]
        s_old = s_scr[h]
        inner = _nt(q, k) * dmask
        o = jnp.dot(inner.astype(BF16), v, preferred_element_type=F32)
        o = o + jnp.dot(q, s_old.astype(BF16), preferred_element_type=F32) * xi
        kz = (k.astype(F32) * zeta).astype(BF16)
        s_scr[h] = s_old * g_chunk + _tn(kz, v)
        y_ref[:, sl] = _group_norm_gate(o, gn_ref[h:h + 1, :], gate_ref[:, sl]).astype(y_ref.dtype)

    @pl.when(c == pl.num_programs(1) - 1)
    def _():
        s_out_ref[...] = s_scr[...]


def _ret_prompt(lg, qr, kr, vr, gr, gn, batch, seq):
    n_chunk = seq // RET_CHUNK
    d_grp = H_RET * D_HEAD
    blk = pl.BlockSpec((RET_CHUNK, d_grp), lambda b, c: (b * n_chunk + c, 0))
    return pl.pallas_call(
        _ret_prompt_kernel,
        grid=(batch, n_chunk),
        in_specs=[
            pl.BlockSpec(memory_space=pltpu.SMEM),
            blk, blk, blk, blk,
            pl.BlockSpec((H_RET, D_HEAD), lambda b, c: (0, 0)),
        ],
        out_specs=[
            blk,
            pl.BlockSpec((None, H_RET, D_HEAD, D_HEAD), lambda b, c: (b, 0, 0, 0)),
        ],
        out_shape=[
            jax.ShapeDtypeStruct((batch * seq, d_grp), BF16),
            jax.ShapeDtypeStruct((batch, H_RET, D_HEAD, D_HEAD), F32),
        ],
        scratch_shapes=[pltpu.VMEM((H_RET, D_HEAD, D_HEAD), F32)],
        compiler_params=_cparams(("parallel", "arbitrary")),
        name="ret_prompt",
    )(lg, qr, kr, vr, gr, gn)


def _ret_sample_kernel(lg_ref, q_ref, k_ref, v_ref, gate_ref, gn_ref, s_ref, y_ref, s_out_ref, *, n_seq, t):
    n_rows = n_seq * t
    seq_of_row = lax.broadcasted_iota(jnp.int32, (n_rows, 1), 0) // t
    for h in range(H_RET):
        sl = slice(h * D_HEAD, (h + 1) * D_HEAD)
        lg = lg_ref[h]
        dmask, xi, zeta, g_chunk = _decay_terms(n_rows, t, lg)
        qf = q_ref[:, sl]
        q = qf.astype(BF16)
        v = v_ref[:, sl].astype(BF16)
        kf = k_ref[:, sl]
        inner = _nt(q, kf.astype(BF16)) * dmask
        o = jnp.dot(inner.astype(BF16), v, preferred_element_type=F32)
        kz = kf * zeta
        cross = []
        for b in range(n_seq):
            s_old = s_ref[b, h]
            q_b = qf[b * t:(b + 1) * t, :].astype(BF16)
            cross.append(jnp.dot(q_b, s_old.astype(BF16), preferred_element_type=F32))
            kz_b = jnp.where(seq_of_row == b, kz, 0.0).astype(BF16)
            s_out_ref[b, h] = s_old * g_chunk + _tn(kz_b, v)
        o = o + jnp.concatenate(cross, axis=0) * xi
        y_ref[:, sl] = _group_norm_gate(o, gn_ref[h:h + 1, :], gate_ref[:, sl]).astype(y_ref.dtype)


def _ret_sample(lg, qr, kr, vr, gr, gn, state, n_batch, t, n_seq):
    d_grp = H_RET * D_HEAD
    blk = pl.BlockSpec((n_seq * t, d_grp), lambda i: (i, 0))
    sblk = pl.BlockSpec((n_seq, H_RET, D_HEAD, D_HEAD), lambda i: (i, 0, 0, 0))
    return pl.pallas_call(
        functools.partial(_ret_sample_kernel, n_seq=n_seq, t=t),
        grid=(n_batch // n_seq,),
        in_specs=[
            pl.BlockSpec(memory_space=pltpu.SMEM),
            blk, blk, blk, blk,
            pl.BlockSpec((H_RET, D_HEAD), lambda i: (0, 0)),
            sblk,
        ],
        out_specs=[blk, sblk],
        out_shape=[
            jax.ShapeDtypeStruct((n_batch * t, d_grp), BF16),
            jax.ShapeDtypeStruct((n_batch, H_RET, D_HEAD, D_HEAD), F32),
        ],
        compiler_params=_cparams(("parallel",)),
        name="ret_sample",
    )(lg, qr, kr, vr, gr, gn, state)


def _lambda(lq1_ref, lk1_ref, lq2_ref, lk2_ref, lam_init):
    a = jnp.sum(lq1_ref[...] * lk1_ref[...], axis=-1, keepdims=True)
    b = jnp.sum(lq2_ref[...] * lk2_ref[...], axis=-1, keepdims=True)
    return jnp.exp(a) - jnp.exp(b) + lam_init


def _diff_finish(o, gain, gate, lam_init):
    y = o * lax.rsqrt(jnp.mean(o * o, axis=-1, keepdims=True) + NORM_EPS)
    return y * gain * (1.0 - lam_init) * _silu(gate)


def _diff_prompt_kernel(slope_ref, q_ref, k_ref, v_ref, gate_ref, gain_ref,
                        lq1_ref, lk1_ref, lq2_ref, lk2_ref, y_ref, *, tq, tk, lam_init):
    h = pl.program_id(1)
    qi = pl.program_id(2)
    slope = slope_ref[h]
    q = q_ref[...]
    q1 = q[:, :DK_DIFF]
    q2 = q[:, DK_DIFF:]
    row = lax.broadcasted_iota(jnp.int32, (tq, tk), 0)
    col = lax.broadcasted_iota(jnp.int32, (tq, tk), 1)
    rel = (col - row).astype(F32) * slope

    def update(carry, s, v):
        m, l, acc = carry
        m_new = jnp.maximum(m, jnp.max(s, axis=-1, keepdims=True))
        corr = jnp.exp(m - m_new)
        p = jnp.exp(s - m_new)
        l = l * corr + jnp.sum(p, axis=-1, keepdims=True)
        acc = acc * corr + jnp.dot(p.astype(BF16), v, preferred_element_type=F32)
        return m_new, l, acc

    def block(ki, carry, masked):
        c1, c2 = carry
        start = pl.multiple_of(ki * tk, tk)
        k = k_ref[pl.ds(start, tk), :]
        v = v_ref[pl.ds(start, tk), :]
        off = slope * (qi * tq - ki * tk).astype(F32)
        bias = rel - off
        if masked:
            bias = jnp.where(col <= row, bias, NEG_INF)
        s1 = _nt(q1, k[:, :DK_DIFF]) + bias
        s2 = _nt(q2, k[:, DK_DIFF:]) + bias
        return update(c1, s1, v), update(c2, s2, v)

    init = (jnp.full((tq, 1), NEG_INF, F32), jnp.zeros((tq, 1), F32), jnp.zeros((tq, D_HEAD), F32))
    n_full = (qi * tq) // tk
    carry = lax.fori_loop(0, n_full, lambda ki, c: block(ki, c, False), (init, init))
    (m1, l1, a1), (m2, l2, a2) = block(n_full, carry, True)
    lam = _lambda(lq1_ref, lk1_ref, lq2_ref, lk2_ref, lam_init)
    o = a1 / l1 - lam * (a2 / l2)
    y_ref[...] = _diff_finish(o, gain_ref[pl.ds(h, 1), :], gate_ref[...], lam_init).astype(y_ref.dtype)


def _diff_prompt(slopes, qd, kd16, vd16, gd, gain, lams, batch, seq, lam_init, tq=256, tk=256):
    assert tq == tk
    nq = seq // tq
    d_grp = H_DIFF * D_HEAD
    qblk = pl.BlockSpec((tq, D_HEAD), lambda b, h, i: (b * nq + i, h))
    kvblk = pl.BlockSpec((seq, D_HEAD), lambda b, h, i: (b, h))
    vec = pl.BlockSpec((1, DK_DIFF), lambda b, h, i: (0, 0))
    return pl.pallas_call(
        functools.partial(_diff_prompt_kernel, tq=tq, tk=tk, lam_init=lam_init),
        grid=(batch, H_DIFF, nq),
        in_specs=[
            pl.BlockSpec(memory_space=pltpu.SMEM),
            qblk, kvblk, kvblk, qblk,
            pl.BlockSpec((H_DIFF, D_HEAD), lambda b, h, i: (0, 0)),
            vec, vec, vec, vec,
        ],
        out_specs=qblk,
        out_shape=jax.ShapeDtypeStruct((batch * seq, d_grp), BF16),
        compiler_params=_cparams(("parallel", "parallel", "arbitrary")),
        name="diff_prompt",
    )(slopes, qd, kd16, vd16, gd, gain, *lams)


def _diff_sample_kernel(pt_ref, q_ref, kl_ref, vl_ref, gate_ref, gain_ref,
                        lq1_ref, lk1_ref, lq2_ref, lk2_ref, *rest, t, n_pp, past, lam_init):
    k_refs = rest[:n_pp]
    v_refs = rest[n_pp:2 * n_pp]
    y_ref = rest[2 * n_pp]
    qbd_ref, kpad_ref, vpad_ref, m_ref, l_ref, acc_ref = rest[2 * n_pp + 1:]
    del pt_ref
    j = pl.program_id(1)
    n_rows = H_DIFF * 2 * t
    d_grp = H_DIFF * D_HEAD

    rowi = lax.broadcasted_iota(jnp.int32, (n_rows, PAGE), 0)
    lane = lax.broadcasted_iota(jnp.int32, (n_rows, PAGE), 1)
    head = rowi // (2 * t)
    tok = rowi % t
    slope = jnp.exp2(-8.0 * (head.astype(F32) + 1.0) / H_DIFF)

    def update(s, v):
        m = m_ref[...]
        m_new = jnp.maximum(m, jnp.max(s, axis=-1, keepdims=True))
        corr = jnp.exp(m - m_new)
        p = jnp.exp(s - m_new)
        l_ref[...] = l_ref[...] * corr + jnp.sum(p, axis=-1, keepdims=True)
        acc_ref[...] = acc_ref[...] * corr + jnp.dot(p.astype(BF16), v, preferred_element_type=F32)
        m_ref[...] = m_new

    @pl.when(j == 0)
    def _():
        q = q_ref[...]
        qt = jnp.concatenate([q] * (n_rows // t), axis=0)
        r = lax.broadcasted_iota(jnp.int32, (n_rows, d_grp), 0)
        c = lax.broadcasted_iota(jnp.int32, (n_rows, d_grp), 1)
        qbd_ref[...] = jnp.where(r // t == c // DK_DIFF, qt, 0.0).astype(qbd_ref.dtype)
        kpad_ref[...] = jnp.zeros_like(kpad_ref)
        vpad_ref[...] = jnp.zeros_like(vpad_ref)
        kpad_ref[0:t, :] = kl_ref[...]
        vpad_ref[0:t, :] = vl_ref[...]
        m_ref[...] = jnp.full_like(m_ref, NEG_INF)
        l_ref[...] = jnp.zeros_like(l_ref)
        acc_ref[...] = jnp.zeros_like(acc_ref)
        dist = (tok - lane).astype(F32)
        bias = jnp.where(lane <= tok, -slope * dist, NEG_INF)
        s = _nt(qbd_ref[...], kpad_ref[...].astype(BF16)) + bias
        update(s, vpad_ref[...].astype(BF16))

    qbd = qbd_ref[...]
    for i in range(n_pp):
        kpos = (j * n_pp + i) * PAGE + lane
        dist = (past + tok - kpos).astype(F32)
        s = _nt(qbd, k_refs[i][...].astype(BF16)) - slope * dist
        update(s, v_refs[i][...].astype(BF16))

    @pl.when(j == pl.num_programs(1) - 1)
    def _():
        lam = _lambda(lq1_ref, lk1_ref, lq2_ref, lk2_ref, lam_init)
        o_all = acc_ref[...] / l_ref[...]
        for h in range(H_DIFF):
            sl = slice(h * D_HEAD, (h + 1) * D_HEAD)
            o1 = o_all[h * 2 * t:h * 2 * t + t, sl]
            o2 = o_all[h * 2 * t + t:(h + 1) * 2 * t, sl]
            o = o1 - lam * o2
            y_ref[:, sl] = _diff_finish(o, gain_ref[h:h + 1, :], gate_ref[:, sl], lam_init).astype(y_ref.dtype)


def _diff_sample(page_table, qd, kd, vd, gd, gain, lams, cache_k, cache_v, n_batch, t, lam_init, n_pp=8):
    n_pages = page_table.shape[1]
    past = n_pages * PAGE
    d_grp = H_DIFF * D_HEAD
    n_rows = H_DIFF * 2 * t
    pt_flat = page_table.reshape(-1)
    tokblk = pl.BlockSpec((t, d_grp), lambda b, j, pt: (b, 0))
    vec = pl.BlockSpec((1, DK_DIFF), lambda b, j, pt: (0, 0))

    def page_spec(i):
        return pl.BlockSpec((None, PAGE, d_grp), lambda b, j, pt: (pt[b * n_pages + j * n_pp + i], 0, 0))

    grid_spec = pltpu.PrefetchScalarGridSpec(
        num_scalar_prefetch=1,
        grid=(n_batch, n_pages // n_pp),
        in_specs=[tokblk, tokblk, tokblk, tokblk,
                  pl.BlockSpec((H_DIFF, D_HEAD), lambda b, j, pt: (0, 0)),
                  vec, vec, vec, vec]
                 + [page_spec(i) for i in range(n_pp)] * 2,
        out_specs=tokblk,
        scratch_shapes=[
            pltpu.VMEM((n_rows, d_grp), BF16),
            pltpu.VMEM((PAGE, d_grp), F32),
            pltpu.VMEM((PAGE, d_grp), F32),
            pltpu.VMEM((n_rows, 1), F32),
            pltpu.VMEM((n_rows, 1), F32),
            pltpu.VMEM((n_rows, d_grp), F32),
        ],
    )
    return pl.pallas_call(
        functools.partial(_diff_sample_kernel, t=t, n_pp=n_pp, past=past, lam_init=lam_init),
        grid_spec=grid_spec,
        out_shape=jax.ShapeDtypeStruct((n_batch * t, d_grp), F32),
        compiler_params=_cparams(("parallel", "arbitrary")),
        name="diff_sample",
    )(pt_flat, qd, kd, vd, gd, gain, *lams, *([cache_k] * n_pp), *([cache_v] * n_pp))


def _finish_kernel(x_ref, yr_ref, yd_ref, w_ref, g_ref, o_ref):
    d_grp = yr_ref.shape[1]
    y = jnp.dot(yr_ref[...].astype(BF16), w_ref[:d_grp, :], preferred_element_type=F32)
    y = y + jnp.dot(yd_ref[...].astype(BF16), w_ref[d_grp:, :], preferred_element_type=F32)
    yn = y * lax.rsqrt(jnp.mean(y * y, axis=-1, keepdims=True) + NORM_EPS) * g_ref[...]
    o_ref[...] = x_ref[...] + yn


def _finish(x, y_ret, y_diff, w_bf16, norm_post, tm):
    n, d_model = x.shape
    d_grp = y_ret.shape[1]
    tok = lambda i: (i, 0)
    return pl.pallas_call(
        _finish_kernel,
        grid=(n // tm,),
        in_specs=[
            pl.BlockSpec((tm, d_model), tok),
            pl.BlockSpec((tm, d_grp), tok),
            pl.BlockSpec((tm, d_grp), tok),
            pl.BlockSpec((2 * d_grp, d_model), lambda i: (0, 0)),
            pl.BlockSpec((1, d_model), lambda i: (0, 0)),
        ],
        out_specs=pl.BlockSpec((tm, d_model), tok),
        out_shape=jax.ShapeDtypeStruct((n, d_model), F32),
        compiler_params=_cparams(("parallel",)),
        name="finish",
    )(x, y_ret, y_diff, w_bf16, norm_post.reshape(1, d_model))


def kernel(x_prompt, x_sample, cache_k, cache_v, page_table, state_ret, norm_pre, norm_post, w_in,
           ret_gn, diff_lq1, diff_lk1, diff_lq2, diff_lk2, diff_norm, w_out):
    batch, seq, d_model = x_prompt.shape
    n_dec, t_dec, _ = x_sample.shape
    depth = w_in.shape[0]
    assert depth == 1
    n_pages = page_table.shape[1]
    past = n_pages * PAGE
    n_pool = cache_k.shape[1]
    d_grp = H_DIFF * D_HEAD
    layer = 0
    lam_init = 0.8 - 0.6 * math.exp(-0.3 * layer)

    lg = jnp.log(1.0 - jnp.exp2(-5.0 - jnp.arange(H_RET, dtype=F32)))
    slopes = jnp.exp2(-8.0 * (jnp.arange(H_DIFF, dtype=F32) + 1.0) / H_DIFF)
    w_in16 = w_in[layer].astype(BF16)
    w_out16 = w_out[layer].astype(BF16)
    lams = [p[layer].reshape(1, DK_DIFF) for p in (diff_lq1, diff_lk1, diff_lq2, diff_lk2)]
    tm = 256

    xp = x_prompt.reshape(batch * seq, d_model)
    cos_p, sin_p = _rope_tables(jnp.arange(seq))
    qr, kr, vr, gr, qd, kd, vd, gd, kd16, vd16 = _project(xp, norm_pre[layer], w_in16, cos_p, sin_p, tm, BF16)
    y_ret, s_fin = _ret_prompt(lg, qr, kr, vr, gr, ret_gn[layer], batch, seq)
    y_diff = _diff_prompt(slopes, qd, kd16, vd16, gd, diff_norm[layer], lams, batch, seq, lam_init)
    out_p = _finish(xp, y_ret, y_diff, w_out16, norm_post[layer], tm)

    xs = x_sample.reshape(n_dec * t_dec, d_model)
    cos_s, sin_s = _rope_tables(past + jnp.arange(tm) % t_dec)
    qr_s, kr_s, vr_s, gr_s, qd_s, kd_s, vd_s, gd_s, _, _ = _project(xs, norm_pre[layer], w_in16, cos_s, sin_s, tm, F32)
    y_ret_s, s_new = _ret_sample(lg, qr_s, kr_s, vr_s, gr_s, ret_gn[layer], state_ret[layer], n_dec, t_dec, n_seq=16)
    ck = cache_k[layer].reshape(n_pool, PAGE, d_grp)
    cv = cache_v[layer].reshape(n_pool, PAGE, d_grp)
    y_diff_s = _diff_sample(page_table, qd_s, kd_s, vd_s, gd_s, diff_norm[layer], lams, ck, cv, n_dec, t_dec, lam_init)
    out_s = _finish(xs, y_ret_s, y_diff_s, w_out16, norm_post[layer], tm)

    return (
        out_p.reshape(batch, seq, d_model),
        out_s.reshape(n_dec, t_dec, d_model),
        kd.reshape(1, batch, seq, H_DIFF, D_HEAD),
        vd.reshape(1, batch, seq, H_DIFF, D_HEAD),
        s_fin.reshape(1, batch, H_RET, D_HEAD, D_HEAD),
        kd_s.reshape(1, n_dec, t_dec, H_DIFF, D_HEAD),
        vd_s.reshape(1, n_dec, t_dec, H_DIFF, D_HEAD),
        s_new.reshape(1, n_dec, H_RET, D_HEAD, D_HEAD),
    )
```

```python
import functools
import math

import jax
import jax.numpy as jnp
from jax import lax
from jax.experimental import pallas as pl
from jax.experimental.pallas import tpu as pltpu

F32 = jnp.float32
BF16 = jnp.bfloat16

H_RET = 4
H_DIFF = 4
D_HEAD = 128
DK_DIFF = 64
RET_CHUNK = 128
PAGE = 128
ROPE_BASE = 10000.0
NORM_EPS = 1e-6
GN_EPS = 1e-5
NEG_INF = float("-inf")

VMEM_LIMIT = 56 * 1024 * 1024


def _cparams(sem):
    return pltpu.CompilerParams(dimension_semantics=sem, vmem_limit_bytes=VMEM_LIMIT)


def _silu(x):
    return x * (1.0 / (1.0 + jnp.exp(-x)))


def _nt(a, b):
    return lax.dot_general(a, b, (((1,), (1,)), ((), ())), preferred_element_type=F32)


def _tn(a, b):
    return lax.dot_general(a, b, (((0,), (0,)), ((), ())), preferred_element_type=F32)


def _proj_kernel(x_ref, g_ref, w_ref, cos_ref, sin_ref,
                 qr_ref, kr_ref, vr_ref, gr_ref, qd_ref, kd_ref, vd_ref, gd_ref, kd16_ref, vdt16_ref):
    d_grp = H_RET * D_HEAD
    x = x_ref[...]
    ms = jnp.mean(x * x, axis=-1, keepdims=True)
    h = (x * lax.rsqrt(ms + NORM_EPS) * g_ref[...]).astype(BF16)
    cos2 = cos_ref[...]
    sin2 = sin_ref[...]

    def section(s):
        return jnp.dot(h, w_ref[:, s * d_grp:(s + 1) * d_grp], preferred_element_type=F32)

    def rope(z, scale):
        for hd in range(H_RET):
            zh = z[:, hd * D_HEAD:(hd + 1) * D_HEAD]
            rot = pltpu.roll(zh, D_HEAD // 2, axis=1)
            yield hd, (zh * cos2 + rot * sin2) * scale

    def store_heads(ref, z):
        if len(ref.shape) == 2:
            ref[...] = z
        else:
            for hd in range(H_DIFF):
                ref[:, hd, :] = z[:, hd * D_HEAD:(hd + 1) * D_HEAD]

    z = section(0)
    for hd, r in rope(z, 1.0):
        qr_ref[:, hd * D_HEAD:(hd + 1) * D_HEAD] = r.astype(qr_ref.dtype)
    z = section(1)
    for hd, r in rope(z, D_HEAD ** -0.5):
        kr_ref[:, hd * D_HEAD:(hd + 1) * D_HEAD] = r.astype(kr_ref.dtype)
    vr_ref[...] = section(2).astype(vr_ref.dtype)
    gr_ref[...] = section(3)
    qd_ref[...] = (section(4) * (DK_DIFF ** -0.5)).astype(qd_ref.dtype)
    z = section(5)
    store_heads(kd_ref, z)
    kd16_ref[...] = z.astype(BF16)
    z = section(6)
    store_heads(vd_ref, z)
    vdt16_ref[...] = z.T.astype(BF16)
    gd_ref[...] = section(7)


def _project(x, norm_pre, w_bf16, cos2, sin2, tm, ret_dtype, batch_seq=None):
    n, d_model = x.shape
    d_in = w_bf16.shape[1]
    d_grp = d_in // 8
    n_tab = cos2.shape[0] // tm
    tok = lambda i: (i, 0)
    out_blk = pl.BlockSpec((tm, d_grp), tok)
    if batch_seq is None:
        kv_shape = jax.ShapeDtypeStruct((n, d_grp), F32)
        kv_blk = out_blk
    else:
        batch, seq = batch_seq
        per_seq = seq // tm
        kv_shape = jax.ShapeDtypeStruct((1, batch, seq, H_DIFF, D_HEAD), F32)
        kv_blk = pl.BlockSpec((None, None, tm, H_DIFF, D_HEAD), lambda i: (0, i // per_seq, i % per_seq, 0, 0))
    shapes = [
        jax.ShapeDtypeStruct((n, d_grp), ret_dtype),
        jax.ShapeDtypeStruct((n, d_grp), ret_dtype),
        jax.ShapeDtypeStruct((n, d_grp), ret_dtype),
        jax.ShapeDtypeStruct((n, d_grp), F32),
        jax.ShapeDtypeStruct((n, d_grp), ret_dtype),
        kv_shape,
        kv_shape,
        jax.ShapeDtypeStruct((n, d_grp), F32),
        jax.ShapeDtypeStruct((n, d_grp), BF16),
        jax.ShapeDtypeStruct((d_grp, n), BF16),
    ]
    out_specs = [out_blk] * 5 + [kv_blk, kv_blk, out_blk, out_blk, pl.BlockSpec((d_grp, tm), lambda i: (0, i))]
    return pl.pallas_call(
        _proj_kernel,
        grid=(n // tm,),
        in_specs=[
            pl.BlockSpec((tm, d_model), tok),
            pl.BlockSpec((1, d_model), lambda i: (0, 0)),
            pl.BlockSpec((d_model, d_in), lambda i: (0, 0)),
            pl.BlockSpec((tm, D_HEAD), lambda i: (i % n_tab, 0)),
            pl.BlockSpec((tm, D_HEAD), lambda i: (i % n_tab, 0)),
        ],
        out_specs=out_specs,
        out_shape=shapes,
        compiler_params=_cparams(("parallel",)),
        name="proj",
    )(x, norm_pre.reshape(1, d_model), w_bf16, cos2, sin2)


def _rope_tables(pos):
    half = D_HEAD // 2
    inv = ROPE_BASE ** (-jnp.arange(half, dtype=F32) / half)
    ang = pos.astype(F32)[:, None] * inv[None, :]
    cos, sin = jnp.cos(ang), jnp.sin(ang)
    return jnp.concatenate([cos, cos], axis=-1), jnp.concatenate([-sin, sin], axis=-1)


def _group_norm_gate(o, gain, gate):
    mu = jnp.mean(o, axis=-1, keepdims=True)
    d = o - mu
    var = jnp.mean(d * d, axis=-1, keepdims=True)
    return d * lax.rsqrt(var + GN_EPS) * gain * _silu(gate)


def _decay_terms(n_rows, chunk, lg):
    row = lax.broadcasted_iota(jnp.int32, (n_rows, n_rows), 0)
    col = lax.broadcasted_iota(jnp.int32, (n_rows, n_rows), 1)
    dist = (row - col).astype(F32)
    dmask = jnp.where(col <= row, jnp.exp(jnp.maximum(dist, 0.0) * lg), 0.0)
    if n_rows != chunk:
        dmask = jnp.where(row // chunk == col // chunk, dmask, 0.0)
    idx = (lax.broadcasted_iota(jnp.int32, (n_rows, 1), 0) % chunk).astype(F32)
    xi = jnp.exp((idx + 1.0) * lg)
    zeta = jnp.exp((chunk - 1.0 - idx) * lg)
    g_chunk = jnp.exp(jnp.full((1, D_HEAD), chunk, F32) * lg)
    return dmask, xi, zeta, g_chunk


def _ret_prompt_kernel(lg_ref, q_ref, k_ref, v_ref, gate_ref, gn_ref, y_ref, s_out_ref, s_scr):
    c = pl.program_id(1)

    @pl.when(c == 0)
    def _():
        s_scr[...] = jnp.zeros_like(s_scr)

    for h in range(H_RET):
        sl = slice(h * D_HEAD, (h + 1) * D_HEAD)
        lg = lg_ref[h]
        dmask, xi, zeta, g_chunk = _decay_terms(RET_CHUNK, RET_CHUNK, lg)
        q = q_ref[:, sl]
        k = k_ref[:, sl]
        v = v_ref[:, sl]
        s_old = s_scr[h]
        inner = _nt(q, k) * dmask
        o = jnp.dot(inner.astype(BF16), v, preferred_element_type=F32)
        o = o + jnp.dot(q, s_old.astype(BF16), preferred_element_type=F32) * xi
        kz = (k.astype(F32) * zeta).astype(BF16)
        s_scr[h] = s_old * g_chunk + _tn(kz, v)
        y_ref[:, sl] = _group_norm_gate(o, gn_ref[h:h + 1, :], gate_ref[:, sl]).astype(y_ref.dtype)

    @pl.when(c == pl.num_programs(1) - 1)
    def _():
        s_out_ref[...] = s_scr[...]


def _ret_prompt(lg, qr, kr, vr, gr, gn, batch, seq):
    n_chunk = seq // RET_CHUNK
    d_grp = H_RET * D_HEAD
    blk = pl.BlockSpec((RET_CHUNK, d_grp), lambda b, c: (b * n_chunk + c, 0))
    return pl.pallas_call(
        _ret_prompt_kernel,
        grid=(batch, n_chunk),
        in_specs=[
            pl.BlockSpec(memory_space=pltpu.SMEM),
            blk, blk, blk, blk,
            pl.BlockSpec((H_RET, D_HEAD), lambda b, c: (0, 0)),
        ],
        out_specs=[
            blk,
            pl.BlockSpec((None, H_RET, D_HEAD, D_HEAD), lambda b, c: (b, 0, 0, 0)),
        ],
        out_shape=[
            jax.ShapeDtypeStruct((batch * seq, d_grp), BF16),
            jax.ShapeDtypeStruct((batch, H_RET, D_HEAD, D_HEAD), F32),
        ],
        scratch_shapes=[pltpu.VMEM((H_RET, D_HEAD, D_HEAD), F32)],
        compiler_params=_cparams(("parallel", "arbitrary")),
        name="ret_prompt",
    )(lg, qr, kr, vr, gr, gn)


def _ret_sample_kernel(lg_ref, q_ref, k_ref, v_ref, gate_ref, gn_ref, s_ref, y_ref, s_out_ref, *, n_seq, t):
    n_rows = n_seq * t
    seq_of_row = lax.broadcasted_iota(jnp.int32, (n_rows, 1), 0) // t
    for h in range(H_RET):
        sl = slice(h * D_HEAD, (h + 1) * D_HEAD)
        lg = lg_ref[h]
        dmask, xi, zeta, g_chunk = _decay_terms(n_rows, t, lg)
        qf = q_ref[:, sl]
        q = qf.astype(BF16)
        v = v_ref[:, sl].astype(BF16)
        kf = k_ref[:, sl]
        inner = _nt(q, kf.astype(BF16)) * dmask
        o = jnp.dot(inner.astype(BF16), v, preferred_element_type=F32)
        kz = kf * zeta
        cross = []
        for b in range(n_seq):
            s_old = s_ref[b, h]
            q_b = qf[b * t:(b + 1) * t, :].astype(BF16)
            cross.append(jnp.dot(q_b, s_old.astype(BF16), preferred_element_type=F32))
            kz_b = jnp.where(seq_of_row == b, kz, 0.0).astype(BF16)
            s_out_ref[b, h] = s_old * g_chunk + _tn(kz_b, v)
        o = o + jnp.concatenate(cross, axis=0) * xi
        y_ref[:, sl] = _group_norm_gate(o, gn_ref[h:h + 1, :], gate_ref[:, sl]).astype(y_ref.dtype)


def _ret_sample(lg, qr, kr, vr, gr, gn, state, n_batch, t, n_seq):
    d_grp = H_RET * D_HEAD
    blk = pl.BlockSpec((n_seq * t, d_grp), lambda i: (i, 0))
    sblk = pl.BlockSpec((n_seq, H_RET, D_HEAD, D_HEAD), lambda i: (i, 0, 0, 0))
    return pl.pallas_call(
        functools.partial(_ret_sample_kernel, n_seq=n_seq, t=t),
        grid=(n_batch // n_seq,),
        in_specs=[
            pl.BlockSpec(memory_space=pltpu.SMEM),
            blk, blk, blk, blk,
            pl.BlockSpec((H_RET, D_HEAD), lambda i: (0, 0)),
            sblk,
        ],
        out_specs=[blk, sblk],
        out_shape=[
            jax.ShapeDtypeStruct((n_batch * t, d_grp), BF16),
            jax.ShapeDtypeStruct((n_batch, H_RET, D_HEAD, D_HEAD), F32),
        ],
        compiler_params=_cparams(("parallel",)),
        name="ret_sample",
    )(lg, qr, kr, vr, gr, gn, state)


def _lambda(lq1_ref, lk1_ref, lq2_ref, lk2_ref, lam_init):
    a = jnp.sum(lq1_ref[...] * lk1_ref[...], axis=-1, keepdims=True)
    b = jnp.sum(lq2_ref[...] * lk2_ref[...], axis=-1, keepdims=True)
    return jnp.exp(a) - jnp.exp(b) + lam_init


def _diff_finish(o, gain, gate, lam_init):
    y = o * lax.rsqrt(jnp.mean(o * o, axis=-1, keepdims=True) + NORM_EPS)
    return y * gain * (1.0 - lam_init) * _silu(gate)


def _map_split(q):
    lane = lax.broadcasted_iota(jnp.int32, q.shape, 1)
    zero = jnp.zeros_like(q)
    return jnp.concatenate([jnp.where(lane < DK_DIFF, q, zero), jnp.where(lane >= DK_DIFF, q, zero)], axis=0)


def _diff_prompt_kernel(slope_ref, q_ref, k_ref, vt_ref, gate_ref, gain_ref,
                        lq1_ref, lk1_ref, lq2_ref, lk2_ref, y_ref, m_ref, l_ref, acc_ref, qbd_ref,
                        sa_ref, sb_ref, *, tq, tk, cw, lam_init):
    h = pl.program_id(1)
    qi = pl.program_id(2)
    slope = slope_ref[h]
    lane_q = lax.broadcasted_iota(jnp.int32, (2 * tq, D_HEAD), 1)
    qbd_ref[:, :D_HEAD] = _map_split(q_ref[...])
    qbd_ref[:, D_HEAD:] = jnp.where(lane_q < 2, 1.0, 0.0).astype(BF16)
    row_k = lax.broadcasted_iota(jnp.int32, (tk, D_HEAD), 0)
    lane_k = lax.broadcasted_iota(jnp.int32, (tk, D_HEAD), 1)
    j_part = jnp.where(lane_k == 0, (row_k // 16) * 16, jnp.where(lane_k == 1, row_k % 16, 0))
    k_aux = (j_part.astype(F32) * slope).astype(BF16)
    kml = lax.broadcasted_iota(jnp.int32, (tk, cw), 0) - lax.broadcasted_iota(jnp.int32, (tk, cw), 1)

    m_ref[...] = jnp.full_like(m_ref, NEG_INF)
    l_ref[...] = jnp.zeros_like(l_ref)
    acc_ref[...] = jnp.zeros_like(acc_ref)

    def scores(ki, s_ref):
        start = pl.multiple_of(ki * tk, tk)
        k_aug = jnp.concatenate([k_ref[pl.ds(start, tk), :], k_aux], axis=1)
        s_ref[...] = _nt(k_aug, qbd_ref[...])

    def absorb(ki, s_ref, diag_off):
        start = pl.multiple_of(ki * tk, tk)
        vt = vt_ref[:, pl.ds(start, tk)]
        for c in range(2 * tq // cw):
            cols = slice(c * cw, (c + 1) * cw)
            q0 = (c * cw) % tq
            if diag_off is not None and diag_off >= q0 + cw:
                continue
            s = s_ref[:, cols]
            if diag_off is not None:
                s = jnp.where(kml <= q0 - diag_off, s, NEG_INF)
            off = slope * (qi * tq - ki * tk).astype(F32)
            m = m_ref[:, cols]
            m_new = jnp.maximum(m, jnp.max(s, axis=0, keepdims=True) - off)
            corr = jnp.exp(m - m_new)
            p = jnp.exp(s - (m_new + off))
            l_ref[:, cols] = l_ref[:, cols] * corr + jnp.sum(p, axis=0, keepdims=True)
            acc_ref[:, cols] = acc_ref[:, cols] * corr + jnp.dot(vt, p.astype(BF16), preferred_element_type=F32)
            m_ref[:, cols] = m_new

    assert tq == 2 * tk
    n_full = 2 * qi
    scores(0, sa_ref)

    def body(pi, carry):
        scores(2 * pi + 1, sb_ref)
        absorb(2 * pi, sa_ref, None)
        scores(2 * pi + 2, sa_ref)
        absorb(2 * pi + 1, sb_ref, None)
        return carry

    lax.fori_loop(0, qi, body, 0)
    scores(n_full + 1, sb_ref)
    absorb(n_full, sa_ref, 0)
    absorb(n_full + 1, sb_ref, tk)

    lam = _lambda(lq1_ref, lk1_ref, lq2_ref, lk2_ref, lam_init)
    o_t = acc_ref[...] / l_ref[...]
    o = (o_t[:, :tq] - lam * o_t[:, tq:]).T
    y_ref[...] = _diff_finish(o, gain_ref[pl.ds(h, 1), :], gate_ref[...], lam_init).astype(y_ref.dtype)


def _diff_prompt(slopes, qd, kd16, vdt16, gd, gain, lams, batch, seq, lam_init, tq=512, tk=256, cw=256):
    assert tq % tk == 0 and tq % cw == 0
    nq = seq // tq
    d_grp = H_DIFF * D_HEAD
    qblk = pl.BlockSpec((tq, D_HEAD), lambda b, h, i: (b * nq + i, h))
    vec = pl.BlockSpec((1, DK_DIFF), lambda b, h, i: (0, 0))
    return pl.pallas_call(
        functools.partial(_diff_prompt_kernel, tq=tq, tk=tk, cw=cw, lam_init=lam_init),
        grid=(batch, H_DIFF, nq),
        in_specs=[
            pl.BlockSpec(memory_space=pltpu.SMEM),
            qblk,
            pl.BlockSpec((seq, D_HEAD), lambda b, h, i: (b, h)),
            pl.BlockSpec((D_HEAD, seq), lambda b, h, i: (h, b)),
            qblk,
            pl.BlockSpec((H_DIFF, D_HEAD), lambda b, h, i: (0, 0)),
            vec, vec, vec, vec,
        ],
        out_specs=qblk,
        out_shape=jax.ShapeDtypeStruct((batch * seq, d_grp), BF16),
        scratch_shapes=[
            pltpu.VMEM((1, 2 * tq), F32),
            pltpu.VMEM((1, 2 * tq), F32),
            pltpu.VMEM((D_HEAD, 2 * tq), F32),
            pltpu.VMEM((2 * tq, 2 * D_HEAD), BF16),
            pltpu.VMEM((tk, 2 * tq), F32),
            pltpu.VMEM((tk, 2 * tq), F32),
        ],
        compiler_params=_cparams(("parallel", "parallel", "arbitrary")),
        name="diff_prompt",
    )(slopes, qd, kd16, vdt16, gd, gain, *lams)


def _diff_sample_kernel(pt_ref, q_ref, kl_ref, vl_ref, gate_ref, gain_ref,
                        lq1_ref, lk1_ref, lq2_ref, lk2_ref, *rest, t, n_pp, n_grp, past, lam_init):
    k_refs = rest[:n_pp]
    v_refs = rest[n_pp:2 * n_pp]
    y_ref = rest[2 * n_pp]
    qall_ref, kpad_ref, vpad_ref, bias_ref, m_ref, l_ref, acc_ref = rest[2 * n_pp + 1:]
    del pt_ref
    j = pl.program_id(1)
    rows_h = 2 * t
    n_rows = H_DIFF * rows_h
    ppg = n_pp // n_grp
    page_cols = PAGE * H_DIFF

    def alibi(width, causal):
        rowi = lax.broadcasted_iota(jnp.int32, (n_rows, width), 0)
        col = lax.broadcasted_iota(jnp.int32, (n_rows, width), 1)
        slope = jnp.exp2(-8.0 * ((rowi // rows_h).astype(F32) + 1.0) / H_DIFF)
        tok_q = rowi % t
        tok_k = col // H_DIFF
        bias = jnp.where(col % H_DIFF == rowi // rows_h, -slope * (tok_q - tok_k).astype(F32), NEG_INF)
        if causal:
            bias = jnp.where(tok_k <= tok_q, bias, NEG_INF)
        return bias

    def update(g, k2, v2, bias, shift):
        sa = _nt(qall_ref[...], k2) + bias
        m = m_ref[g]
        m_new = jnp.maximum(m, jnp.max(sa, axis=-1, keepdims=True) - shift)
        corr = jnp.exp(m - m_new)
        p = jnp.exp(sa - (m_new + shift))
        l_ref[g] = l_ref[g] * corr + jnp.sum(p, axis=-1, keepdims=True)
        acc_ref[g] = acc_ref[g] * corr + jnp.dot(p.astype(BF16), v2, preferred_element_type=F32)
        m_ref[g] = m_new

    @pl.when(j == 0)
    def _():
        qall_ref[...] = jnp.concatenate(
            [_map_split(q_ref[:, h * D_HEAD:(h + 1) * D_HEAD]) for h in range(H_DIFF)], axis=0).astype(BF16)
        kpad_ref[...] = jnp.zeros_like(kpad_ref)
        vpad_ref[...] = jnp.zeros_like(vpad_ref)
        for tk in range(t):
            for h in range(H_DIFF):
                r = tk * H_DIFF + h
                kpad_ref[r:r + 1, :] = kl_ref[tk:tk + 1, h * D_HEAD:(h + 1) * D_HEAD]
                vpad_ref[r:r + 1, :] = vl_ref[tk:tk + 1, h * D_HEAD:(h + 1) * D_HEAD]
        bias_ref[...] = alibi(ppg * page_cols, causal=False)
        m_ref[...] = jnp.full_like(m_ref, NEG_INF)
        l_ref[...] = jnp.zeros_like(l_ref)
        acc_ref[...] = jnp.zeros_like(acc_ref)
        update(0, kpad_ref[...].astype(BF16), vpad_ref[...].astype(BF16),
               alibi(page_cols, causal=True), jnp.zeros((n_rows, 1), F32))

    rowi = lax.broadcasted_iota(jnp.int32, (n_rows, 1), 0)
    slope = jnp.exp2(-8.0 * ((rowi // rows_h).astype(F32) + 1.0) / H_DIFF)
    for g in range(n_grp):
        pages = range(g * ppg, (g + 1) * ppg)
        first_key = (j * n_pp + g * ppg) * PAGE
        shift = slope * (past - first_key).astype(F32)
        k2 = jnp.concatenate([k_refs[i][...].astype(BF16) for i in pages], axis=0)
        v2 = jnp.concatenate([v_refs[i][...].astype(BF16) for i in pages], axis=0)
        update(g, k2, v2, bias_ref[...], shift)

    @pl.when(j == pl.num_programs(1) - 1)
    def _():
        m_all = m_ref[0]
        for g in range(1, n_grp):
            m_all = jnp.maximum(m_all, m_ref[g])
        l_all = jnp.zeros_like(m_all)
        acc = jnp.zeros((n_rows, D_HEAD), F32)
        for g in range(n_grp):
            w = jnp.exp(m_ref[g] - m_all)
            l_all = l_all + l_ref[g] * w
            acc = acc + acc_ref[g] * w
        o_all = acc / l_all
        lam = _lambda(lq1_ref, lk1_ref, lq2_ref, lk2_ref, lam_init)
        for h in range(H_DIFF):
            sl = slice(h * D_HEAD, (h + 1) * D_HEAD)
            o = o_all[h * rows_h:h * rows_h + t, :] - lam * o_all[h * rows_h + t:(h + 1) * rows_h, :]
            y_ref[:, sl] = _diff_finish(o, gain_ref[h:h + 1, :], gate_ref[:, sl], lam_init).astype(y_ref.dtype)


def _diff_sample(page_table, qd, kd, vd, gd, gain, lams, cache_k, cache_v, n_batch, t, lam_init, n_pp=16, n_grp=2):
    n_pages = page_table.shape[1]
    past = n_pages * PAGE
    d_grp = H_DIFF * D_HEAD
    n_rows = H_DIFF * 2 * t
    pt_flat = page_table.reshape(-1)
    tokblk = pl.BlockSpec((t, d_grp), lambda b, j, pt: (b, 0))
    vec = pl.BlockSpec((1, DK_DIFF), lambda b, j, pt: (0, 0))
    page_cols = PAGE * H_DIFF
    n_pool = cache_k.shape[1]
    cache_k = cache_k.reshape(n_pool, page_cols, D_HEAD)
    cache_v = cache_v.reshape(n_pool, page_cols, D_HEAD)

    def page_spec(i):
        return pl.BlockSpec((None, page_cols, D_HEAD), lambda b, j, pt: (pt[b * n_pages + j * n_pp + i], 0, 0))

    grid_spec = pltpu.PrefetchScalarGridSpec(
        num_scalar_prefetch=1,
        grid=(n_batch, n_pages // n_pp),
        in_specs=[tokblk, tokblk, tokblk, tokblk,
                  pl.BlockSpec((H_DIFF, D_HEAD), lambda b, j, pt: (0, 0)),
                  vec, vec, vec, vec]
                 + [page_spec(i) for i in range(n_pp)] * 2,
        out_specs=tokblk,
        scratch_shapes=[
            pltpu.VMEM((n_rows, D_HEAD), BF16),
            pltpu.VMEM((page_cols, D_HEAD), F32),
            pltpu.VMEM((page_cols, D_HEAD), F32),
            pltpu.VMEM((n_rows, (n_pp // n_grp) * page_cols), F32),
            pltpu.VMEM((n_grp, n_rows, 1), F32),
            pltpu.VMEM((n_grp, n_rows, 1), F32),
            pltpu.VMEM((n_grp, n_rows, D_HEAD), F32),
        ],
    )
    return pl.pallas_call(
        functools.partial(_diff_sample_kernel, t=t, n_pp=n_pp, n_grp=n_grp, past=past, lam_init=lam_init),
        grid_spec=grid_spec,
        out_shape=jax.ShapeDtypeStruct((n_batch * t, d_grp), F32),
        compiler_params=_cparams(("parallel", "arbitrary")),
        name="diff_sample",
    )(pt_flat, qd, kd, vd, gd, gain, *lams, *([cache_k] * n_pp), *([cache_v] * n_pp))


def _finish_kernel(x_ref, yr_ref, yd_ref, w_ref, g_ref, o_ref):
    d_grp = yr_ref.shape[1]
    y = jnp.dot(yr_ref[...].astype(BF16), w_ref[:d_grp, :], preferred_element_type=F32)
    y = y + jnp.dot(yd_ref[...].astype(BF16), w_ref[d_grp:, :], preferred_element_type=F32)
    yn = y * lax.rsqrt(jnp.mean(y * y, axis=-1, keepdims=True) + NORM_EPS) * g_ref[...]
    o_ref[...] = x_ref[...] + yn


def _finish(x, y_ret, y_diff, w_bf16, norm_post, tm):
    n, d_model = x.shape
    d_grp = y_ret.shape[1]
    tok = lambda i: (i, 0)
    return pl.pallas_call(
        _finish_kernel,
        grid=(n // tm,),
        in_specs=[
            pl.BlockSpec((tm, d_model), tok),
            pl.BlockSpec((tm, d_grp), tok),
            pl.BlockSpec((tm, d_grp), tok),
            pl.BlockSpec((2 * d_grp, d_model), lambda i: (0, 0)),
            pl.BlockSpec((1, d_model), lambda i: (0, 0)),
        ],
        out_specs=pl.BlockSpec((tm, d_model), tok),
        out_shape=jax.ShapeDtypeStruct((n, d_model), F32),
        compiler_params=_cparams(("parallel",)),
        name="finish",
    )(x, y_ret, y_diff, w_bf16, norm_post.reshape(1, d_model))


def kernel(x_prompt, x_sample, cache_k, cache_v, page_table, state_ret, norm_pre, norm_post, w_in,
           ret_gn, diff_lq1, diff_lk1, diff_lq2, diff_lk2, diff_norm, w_out):
    batch, seq, d_model = x_prompt.shape
    n_dec, t_dec, _ = x_sample.shape
    depth = w_in.shape[0]
    assert depth == 1
    n_pages = page_table.shape[1]
    past = n_pages * PAGE
    layer = 0
    lam_init = 0.8 - 0.6 * math.exp(-0.3 * layer)

    lg = jnp.log(1.0 - jnp.exp2(-5.0 - jnp.arange(H_RET, dtype=F32)))
    assert 8 % H_DIFF == 0
    slope_exponents = -(8 // H_DIFF) * (jnp.arange(H_DIFF, dtype=jnp.int32) + 1)
    slopes = lax.bitcast_convert_type((slope_exponents + 127) << 23, F32)
    w_in16 = w_in[layer].astype(BF16)
    w_out16 = w_out[layer].astype(BF16)
    lams = [p[layer].reshape(1, DK_DIFF) for p in (diff_lq1, diff_lk1, diff_lq2, diff_lk2)]
    tm = 256

    xp = x_prompt.reshape(batch * seq, d_model)
    cos_p, sin_p = _rope_tables(jnp.arange(seq))
    qr, kr, vr, gr, qd, kd, vd, gd, kd16, vdt16 = _project(
        xp, norm_pre[layer], w_in16, cos_p, sin_p, tm, BF16, batch_seq=(batch, seq))
    y_ret, s_fin = _ret_prompt(lg, qr, kr, vr, gr, ret_gn[layer], batch, seq)
    y_diff = _diff_prompt(slopes, qd, kd16, vdt16, gd, diff_norm[layer], lams, batch, seq, lam_init)
    out_p = _finish(xp, y_ret, y_diff, w_out16, norm_post[layer], tm)

    xs = x_sample.reshape(n_dec * t_dec, d_model)
    cos_s, sin_s = _rope_tables(past + jnp.arange(tm) % t_dec)
    qr_s, kr_s, vr_s, gr_s, qd_s, kd_s, vd_s, gd_s, _, _ = _project(xs, norm_pre[layer], w_in16, cos_s, sin_s, tm, F32)
    y_ret_s, s_new = _ret_sample(lg, qr_s, kr_s, vr_s, gr_s, ret_gn[layer], state_ret[layer], n_dec, t_dec, n_seq=16)
    y_diff_s = _diff_sample(page_table, qd_s, kd_s, vd_s, gd_s, diff_norm[layer], lams, cache_k, cache_v,
                            n_dec, t_dec, lam_init)
    out_s = _finish(xs, y_ret_s, y_diff_s, w_out16, norm_post[layer], tm)

    return (
        out_p.reshape(batch, seq, d_model),
        out_s.reshape(n_dec, t_dec, d_model),
        kd,
        vd,
        s_fin.reshape(1, batch, H_RET, D_HEAD, D_HEAD),
        kd_s.reshape(1, n_dec, t_dec, H_DIFF, D_HEAD),
        vd_s.reshape(1, n_dec, t_dec, H_DIFF, D_HEAD),
        s_new.reshape(1, n_dec, H_RET, D_HEAD, D_HEAD),
    )
```

```python
import functools
import math

import jax
import jax.numpy as jnp
from jax import lax
from jax.experimental import pallas as pl
from jax.experimental.pallas import tpu as pltpu

F32 = jnp.float32
BF16 = jnp.bfloat16

H_RET = 4
H_DIFF = 4
D_HEAD = 128
DK_DIFF = 64
RET_CHUNK = 128
PAGE = 128
ROPE_BASE = 10000.0
NORM_EPS = 1e-6
GN_EPS = 1e-5
NEG_INF = float("-inf")

VMEM_LIMIT = 56 * 1024 * 1024


def _cparams(sem):
    return pltpu.CompilerParams(dimension_semantics=sem, vmem_limit_bytes=VMEM_LIMIT)


def _silu(x):
    return x * (1.0 / (1.0 + jnp.exp(-x)))


def _nt(a, b):
    return lax.dot_general(a, b, (((1,), (1,)), ((), ())), preferred_element_type=F32)


def _tn(a, b):
    return lax.dot_general(a, b, (((0,), (0,)), ((), ())), preferred_element_type=F32)


def _proj_kernel(x_ref, g_ref, w_ref, cos_ref, sin_ref,
                 qr_ref, kr_ref, vr_ref, gr_ref, qd_ref, kd_ref, vd_ref, gd_ref, kd16_ref, vdt16_ref):
    d_grp = H_RET * D_HEAD
    x = x_ref[...]
    ms = jnp.mean(x * x, axis=-1, keepdims=True)
    h = (x * lax.rsqrt(ms + NORM_EPS) * g_ref[...]).astype(BF16)
    cos2 = cos_ref[...]
    sin2 = sin_ref[...]

    def section(s):
        return jnp.dot(h, w_ref[:, s * d_grp:(s + 1) * d_grp], preferred_element_type=F32)

    def rope(z, scale):
        for hd in range(H_RET):
            zh = z[:, hd * D_HEAD:(hd + 1) * D_HEAD]
            rot = pltpu.roll(zh, D_HEAD // 2, axis=1)
            yield hd, (zh * cos2 + rot * sin2) * scale

    def store_heads(ref, z):
        if len(ref.shape) == 2:
            ref[...] = z
        else:
            for hd in range(H_DIFF):
                ref[:, hd, :] = z[:, hd * D_HEAD:(hd + 1) * D_HEAD]

    z = section(0)
    for hd, r in rope(z, 1.0):
        qr_ref[:, hd * D_HEAD:(hd + 1) * D_HEAD] = r.astype(qr_ref.dtype)
    z = section(1)
    for hd, r in rope(z, D_HEAD ** -0.5):
        kr_ref[:, hd * D_HEAD:(hd + 1) * D_HEAD] = r.astype(kr_ref.dtype)
    vr_ref[...] = section(2).astype(vr_ref.dtype)
    gr_ref[...] = section(3)
    qd_ref[...] = (section(4) * (DK_DIFF ** -0.5)).astype(qd_ref.dtype)
    z = section(5)
    store_heads(kd_ref, z)
    kd16_ref[...] = z.astype(BF16)
    z = section(6)
    store_heads(vd_ref, z)
    vdt16_ref[...] = z.T.astype(BF16)
    gd_ref[...] = section(7)


def _project(x, norm_pre, w_bf16, cos2, sin2, tm, ret_dtype, batch_seq=None):
    n, d_model = x.shape
    d_in = w_bf16.shape[1]
    d_grp = d_in // 8
    n_tab = cos2.shape[0] // tm
    tok = lambda i: (i, 0)
    out_blk = pl.BlockSpec((tm, d_grp), tok)
    if batch_seq is None:
        kv_shape = jax.ShapeDtypeStruct((n, d_grp), F32)
        kv_blk = out_blk
    else:
        batch, seq = batch_seq
        per_seq = seq // tm
        kv_shape = jax.ShapeDtypeStruct((1, batch, seq, H_DIFF, D_HEAD), F32)
        kv_blk = pl.BlockSpec((None, None, tm, H_DIFF, D_HEAD), lambda i: (0, i // per_seq, i % per_seq, 0, 0))
    shapes = [
        jax.ShapeDtypeStruct((n, d_grp), ret_dtype),
        jax.ShapeDtypeStruct((n, d_grp), ret_dtype),
        jax.ShapeDtypeStruct((n, d_grp), ret_dtype),
        jax.ShapeDtypeStruct((n, d_grp), F32),
        jax.ShapeDtypeStruct((n, d_grp), ret_dtype),
        kv_shape,
        kv_shape,
        jax.ShapeDtypeStruct((n, d_grp), F32),
        jax.ShapeDtypeStruct((n, d_grp), BF16),
        jax.ShapeDtypeStruct((d_grp, n), BF16),
    ]
    out_specs = [out_blk] * 5 + [kv_blk, kv_blk, out_blk, out_blk, pl.BlockSpec((d_grp, tm), lambda i: (0, i))]
    return pl.pallas_call(
        _proj_kernel,
        grid=(n // tm,),
        in_specs=[
            pl.BlockSpec((tm, d_model), tok),
            pl.BlockSpec((1, d_model), lambda i: (0, 0)),
            pl.BlockSpec((d_model, d_in), lambda i: (0, 0)),
            pl.BlockSpec((tm, D_HEAD), lambda i: (i % n_tab, 0)),
            pl.BlockSpec((tm, D_HEAD), lambda i: (i % n_tab, 0)),
        ],
        out_specs=out_specs,
        out_shape=shapes,
        compiler_params=_cparams(("parallel",)),
        name="proj",
    )(x, norm_pre.reshape(1, d_model), w_bf16, cos2, sin2)


def _rope_tables(pos):
    half = D_HEAD // 2
    inv = ROPE_BASE ** (-jnp.arange(half, dtype=F32) / half)
    ang = pos.astype(F32)[:, None] * inv[None, :]
    cos, sin = jnp.cos(ang), jnp.sin(ang)
    return jnp.concatenate([cos, cos], axis=-1), jnp.concatenate([-sin, sin], axis=-1)


def _group_norm_gate(o, gain, gate):
    mu = jnp.mean(o, axis=-1, keepdims=True)
    d = o - mu
    var = jnp.mean(d * d, axis=-1, keepdims=True)
    return d * lax.rsqrt(var + GN_EPS) * gain * _silu(gate)


def _decay_terms(n_rows, chunk, lg):
    row = lax.broadcasted_iota(jnp.int32, (n_rows, n_rows), 0)
    col = lax.broadcasted_iota(jnp.int32, (n_rows, n_rows), 1)
    dist = (row - col).astype(F32)
    dmask = jnp.where(col <= row, jnp.exp(jnp.maximum(dist, 0.0) * lg), 0.0)
    if n_rows != chunk:
        dmask = jnp.where(row // chunk == col // chunk, dmask, 0.0)
    idx = (lax.broadcasted_iota(jnp.int32, (n_rows, 1), 0) % chunk).astype(F32)
    xi = jnp.exp((idx + 1.0) * lg)
    zeta = jnp.exp((chunk - 1.0 - idx) * lg)
    g_chunk = jnp.exp(jnp.full((1, D_HEAD), chunk, F32) * lg)
    return dmask, xi, zeta, g_chunk


def _ret_prompt_kernel(lg_ref, q_ref, k_ref, v_ref, gate_ref, gn_ref, y_ref, s_out_ref, s_scr):
    c = pl.program_id(1)

    @pl.when(c == 0)
    def _():
        s_scr[...] = jnp.zeros_like(s_scr)

    for h in range(H_RET):
        sl = slice(h * D_HEAD, (h + 1) * D_HEAD)
        lg = lg_ref[h]
        dmask, xi, zeta, g_chunk = _decay_terms(RET_CHUNK, RET_CHUNK, lg)
        q = q_ref[:, sl]
        k = k_ref[:, sl]
        v = v_ref[:, sl]
        s_old = s_scr[h]
        inner = _nt(q, k) * dmask
        o = jnp.dot(inner.astype(BF16), v, preferred_element_type=F32)
        o = o + jnp.dot(q, s_old.astype(BF16), preferred_element_type=F32) * xi
        kz = (k.astype(F32) * zeta).astype(BF16)
        s_scr[h] = s_old * g_chunk + _tn(kz, v)
        y_ref[:, sl] = _group_norm_gate(o, gn_ref[h:h + 1, :], gate_ref[:, sl]).astype(y_ref.dtype)

    @pl.when(c == pl.num_programs(1) - 1)
    def _():
        s_out_ref[...] = s_scr[...]


def _ret_prompt(lg, qr, kr, vr, gr, gn, batch, seq):
    n_chunk = seq // RET_CHUNK
    d_grp = H_RET * D_HEAD
    blk = pl.BlockSpec((RET_CHUNK, d_grp), lambda b, c: (b * n_chunk + c, 0))
    return pl.pallas_call(
        _ret_prompt_kernel,
        grid=(batch, n_chunk),
        in_specs=[
            pl.BlockSpec(memory_space=pltpu.SMEM),
            blk, blk, blk, blk,
            pl.BlockSpec((H_RET, D_HEAD), lambda b, c: (0, 0)),
        ],
        out_specs=[
            blk,
            pl.BlockSpec((None, H_RET, D_HEAD, D_HEAD), lambda b, c: (b, 0, 0, 0)),
        ],
        out_shape=[
            jax.ShapeDtypeStruct((batch * seq, d_grp), BF16),
            jax.ShapeDtypeStruct((batch, H_RET, D_HEAD, D_HEAD), F32),
        ],
        scratch_shapes=[pltpu.VMEM((H_RET, D_HEAD, D_HEAD), F32)],
        compiler_params=_cparams(("parallel", "arbitrary")),
        name="ret_prompt",
    )(lg, qr, kr, vr, gr, gn)


def _ret_sample_kernel(lg_ref, q_ref, k_ref, v_ref, gate_ref, gn_ref, s_ref, y_ref, s_out_ref, *, n_seq, t):
    n_rows = n_seq * t
    seq_of_row = lax.broadcasted_iota(jnp.int32, (n_rows, 1), 0) // t
    for h in range(H_RET):
        sl = slice(h * D_HEAD, (h + 1) * D_HEAD)
        lg = lg_ref[h]
        dmask, xi, zeta, g_chunk = _decay_terms(n_rows, t, lg)
        qf = q_ref[:, sl]
        q = qf.astype(BF16)
        v = v_ref[:, sl].astype(BF16)
        kf = k_ref[:, sl]
        inner = _nt(q, kf.astype(BF16)) * dmask
        o = jnp.dot(inner.astype(BF16), v, preferred_element_type=F32)
        kz = kf * zeta
        cross = []
        for b in range(n_seq):
            s_old = s_ref[b, h]
            q_b = qf[b * t:(b + 1) * t, :].astype(BF16)
            cross.append(jnp.dot(q_b, s_old.astype(BF16), preferred_element_type=F32))
            kz_b = jnp.where(seq_of_row == b, kz, 0.0).astype(BF16)
            s_out_ref[b, h] = s_old * g_chunk + _tn(kz_b, v)
        o = o + jnp.concatenate(cross, axis=0) * xi
        y_ref[:, sl] = _group_norm_gate(o, gn_ref[h:h + 1, :], gate_ref[:, sl]).astype(y_ref.dtype)


def _ret_sample(lg, qr, kr, vr, gr, gn, state, n_batch, t, n_seq):
    d_grp = H_RET * D_HEAD
    blk = pl.BlockSpec((n_seq * t, d_grp), lambda i: (i, 0))
    sblk = pl.BlockSpec((n_seq, H_RET, D_HEAD, D_HEAD), lambda i: (i, 0, 0, 0))
    return pl.pallas_call(
        functools.partial(_ret_sample_kernel, n_seq=n_seq, t=t),
        grid=(n_batch // n_seq,),
        in_specs=[
            pl.BlockSpec(memory_space=pltpu.SMEM),
            blk, blk, blk, blk,
            pl.BlockSpec((H_RET, D_HEAD), lambda i: (0, 0)),
            sblk,
        ],
        out_specs=[blk, sblk],
        out_shape=[
            jax.ShapeDtypeStruct((n_batch * t, d_grp), BF16),
            jax.ShapeDtypeStruct((n_batch, H_RET, D_HEAD, D_HEAD), F32),
        ],
        compiler_params=_cparams(("parallel",)),
        name="ret_sample",
    )(lg, qr, kr, vr, gr, gn, state)


def _lambda(lq1_ref, lk1_ref, lq2_ref, lk2_ref, lam_init):
    a = jnp.sum(lq1_ref[...] * lk1_ref[...], axis=-1, keepdims=True)
    b = jnp.sum(lq2_ref[...] * lk2_ref[...], axis=-1, keepdims=True)
    return jnp.exp(a) - jnp.exp(b) + lam_init


def _diff_finish(o, gain, gate, lam_init):
    y = o * lax.rsqrt(jnp.mean(o * o, axis=-1, keepdims=True) + NORM_EPS)
    return y * gain * (1.0 - lam_init) * _silu(gate)


def _map_split(q):
    lane = lax.broadcasted_iota(jnp.int32, q.shape, 1)
    zero = jnp.zeros_like(q)
    return jnp.concatenate([jnp.where(lane < DK_DIFF, q, zero), jnp.where(lane >= DK_DIFF, q, zero)], axis=0)


def _diff_prompt_kernel(slope_ref, q_ref, k_ref, vt_ref, gate_ref, gain_ref,
                        lq1_ref, lk1_ref, lq2_ref, lk2_ref, y_ref, m_ref, l_ref, acc_ref, qbd_ref,
                        sa_ref, sb_ref, *, tq, tk, cw, lam_init):
    h = pl.program_id(1)
    qi = pl.program_id(2)
    slope = slope_ref[h]
    lane_q = lax.broadcasted_iota(jnp.int32, (2 * tq, D_HEAD), 1)
    qbd_ref[:, :D_HEAD] = _map_split(q_ref[...])
    qbd_ref[:, D_HEAD:] = jnp.where(lane_q < 2, 1.0, 0.0).astype(BF16)
    row_k = lax.broadcasted_iota(jnp.int32, (tk, D_HEAD), 0)
    lane_k = lax.broadcasted_iota(jnp.int32, (tk, D_HEAD), 1)
    j_part = jnp.where(lane_k == 0, (row_k // 16) * 16, jnp.where(lane_k == 1, row_k % 16, 0))
    k_aux = (j_part.astype(F32) * slope).astype(BF16)
    kml = lax.broadcasted_iota(jnp.int32, (tk, cw), 0) - lax.broadcasted_iota(jnp.int32, (tk, cw), 1)

    m_ref[...] = jnp.full_like(m_ref, NEG_INF)
    l_ref[...] = jnp.zeros_like(l_ref)
    acc_ref[...] = jnp.zeros_like(acc_ref)

    def scores(ki, s_ref):
        start = pl.multiple_of(ki * tk, tk)
        k_aug = jnp.concatenate([k_ref[pl.ds(start, tk), :], k_aux], axis=1)
        s_ref[...] = _nt(k_aug, qbd_ref[...])

    def absorb(ki, s_ref, diag_off):
        start = pl.multiple_of(ki * tk, tk)
        vt = vt_ref[:, pl.ds(start, tk)]
        for c in range(2 * tq // cw):
            cols = slice(c * cw, (c + 1) * cw)
            q0 = (c * cw) % tq
            if diag_off is not None and diag_off >= q0 + cw:
                continue
            s = s_ref[:, cols]
            if diag_off is not None:
                s = jnp.where(kml <= q0 - diag_off, s, NEG_INF)
            off = slope * (qi * tq - ki * tk).astype(F32)
            m = m_ref[:, cols]
            m_new = jnp.maximum(m, jnp.max(s, axis=0, keepdims=True) - off)
            corr = jnp.exp(m - m_new)
            p = jnp.exp(s - (m_new + off))
            l_ref[:, cols] = l_ref[:, cols] * corr + jnp.sum(p, axis=0, keepdims=True)
            acc_ref[:, cols] = acc_ref[:, cols] * corr + jnp.dot(vt, p.astype(BF16), preferred_element_type=F32)
            m_ref[:, cols] = m_new

    assert tq == 2 * tk
    n_full = 2 * qi
    scores(0, sa_ref)

    def body(pi, carry):
        scores(2 * pi + 1, sb_ref)
        absorb(2 * pi, sa_ref, None)
        scores(2 * pi + 2, sa_ref)
        absorb(2 * pi + 1, sb_ref, None)
        return carry

    lax.fori_loop(0, qi, body, 0)
    scores(n_full + 1, sb_ref)
    absorb(n_full, sa_ref, 0)
    absorb(n_full + 1, sb_ref, tk)

    lam = _lambda(lq1_ref, lk1_ref, lq2_ref, lk2_ref, lam_init)
    o_t = acc_ref[...] / l_ref[...]
    o = (o_t[:, :tq] - lam * o_t[:, tq:]).T
    y_ref[...] = _diff_finish(o, gain_ref[pl.ds(h, 1), :], gate_ref[...], lam_init).astype(y_ref.dtype)


def _diff_prompt(slopes, qd, kd16, vdt16, gd, gain, lams, batch, seq, lam_init, tq=512, tk=256, cw=256):
    assert tq % tk == 0 and tq % cw == 0
    nq = seq // tq
    d_grp = H_DIFF * D_HEAD
    qblk = pl.BlockSpec((tq, D_HEAD), lambda b, h, i: (b * nq + i, h))
    vec = pl.BlockSpec((1, DK_DIFF), lambda b, h, i: (0, 0))
    return pl.pallas_call(
        functools.partial(_diff_prompt_kernel, tq=tq, tk=tk, cw=cw, lam_init=lam_init),
        grid=(batch, H_DIFF, nq),
        in_specs=[
            pl.BlockSpec(memory_space=pltpu.SMEM),
            qblk,
            pl.BlockSpec((seq, D_HEAD), lambda b, h, i: (b, h)),
            pl.BlockSpec((D_HEAD, seq), lambda b, h, i: (h, b)),
            qblk,
            pl.BlockSpec((H_DIFF, D_HEAD), lambda b, h, i: (0, 0)),
            vec, vec, vec, vec,
        ],
        out_specs=qblk,
        out_shape=jax.ShapeDtypeStruct((batch * seq, d_grp), BF16),
        scratch_shapes=[
            pltpu.VMEM((1, 2 * tq), F32),
            pltpu.VMEM((1, 2 * tq), F32),
            pltpu.VMEM((D_HEAD, 2 * tq), F32),
            pltpu.VMEM((2 * tq, 2 * D_HEAD), BF16),
            pltpu.VMEM((tk, 2 * tq), F32),
            pltpu.VMEM((tk, 2 * tq), F32),
        ],
        compiler_params=_cparams(("parallel", "parallel", "arbitrary")),
        name="diff_prompt",
    )(slopes, qd, kd16, vdt16, gd, gain, *lams)


def _diff_sample_kernel(pt_ref, q_ref, kl_ref, vl_ref, gate_ref, gain_ref,
                        lq1_ref, lk1_ref, lq2_ref, lk2_ref, ck_hbm, cv_hbm, y_ref,
                        qall_ref, kpad_ref, vpad_ref, bias_ref, m_ref, l_ref, acc_ref, kbuf, vbuf, sem,
                        *, t, n_pp, n_grp, n_slots, past, lam_init):
    j = pl.program_id(1)
    n_steps = pl.num_programs(1)
    step = pl.program_id(0) * n_steps + j
    n_total = pl.num_programs(0) * n_steps
    rows_h = 2 * t
    n_rows = H_DIFF * rows_h
    ppg = n_pp // n_grp

    def batch_copies(src_step, slot, known_pages):
        for i in range(n_pp):
            page = pt_ref[src_step * n_pp + i] if known_pages else 0
            yield pltpu.make_async_copy(ck_hbm.at[page], kbuf.at[slot, i], sem.at[slot])
            yield pltpu.make_async_copy(cv_hbm.at[page], vbuf.at[slot, i], sem.at[slot])

    def start_batch(src_step):
        for cp in batch_copies(src_step, src_step % n_slots, True):
            cp.start()

    @pl.when(step == 0)
    def _():
        for d in range(n_slots - 1):
            start_batch(d)

    @pl.when(step + (n_slots - 1) < n_total)
    def _():
        start_batch(step + (n_slots - 1))

    slot = step % n_slots
    for cp in batch_copies(step, slot, False):
        cp.wait()
    k_refs = [kbuf.at[slot, i] for i in range(n_pp)]
    v_refs = [vbuf.at[slot, i] for i in range(n_pp)]

    def alibi(width, causal):
        rowi = lax.broadcasted_iota(jnp.int32, (n_rows, width), 0)
        tok_k = lax.broadcasted_iota(jnp.int32, (n_rows, width), 1)
        slope = jnp.exp2(-8.0 * ((rowi // rows_h).astype(F32) + 1.0) / H_DIFF)
        tok_q = rowi % t
        bias = -slope * (tok_q - tok_k).astype(F32)
        if causal:
            bias = jnp.where(tok_k <= tok_q, bias, NEG_INF)
        return bias

    def update(g, k_heads, v_heads, bias, shift):
        sa = jnp.concatenate(
            [_nt(qall_ref[h * rows_h:(h + 1) * rows_h, :], k_heads[h]) for h in range(H_DIFF)], axis=0) + bias
        m = m_ref[g]
        m_new = jnp.maximum(m, jnp.max(sa, axis=-1, keepdims=True) - shift)
        corr = jnp.exp(m - m_new)
        p = jnp.exp(sa - (m_new + shift))
        l_ref[g] = l_ref[g] * corr + jnp.sum(p, axis=-1, keepdims=True)
        pv = jnp.concatenate(
            [jnp.dot(p[h * rows_h:(h + 1) * rows_h, :].astype(BF16), v_heads[h], preferred_element_type=F32)
             for h in range(H_DIFF)], axis=0)
        acc_ref[g] = acc_ref[g] * corr + pv
        m_ref[g] = m_new

    @pl.when(j == 0)
    def _():
        qall_ref[...] = jnp.concatenate(
            [_map_split(q_ref[:, h * D_HEAD:(h + 1) * D_HEAD]) for h in range(H_DIFF)], axis=0).astype(BF16)
        kpad_ref[...] = jnp.zeros_like(kpad_ref)
        vpad_ref[...] = jnp.zeros_like(vpad_ref)
        kpad_ref[0:t, :] = kl_ref[...]
        vpad_ref[0:t, :] = vl_ref[...]
        bias_ref[...] = alibi(ppg * PAGE, causal=False)
        m_ref[...] = jnp.full_like(m_ref, NEG_INF)
        l_ref[...] = jnp.zeros_like(l_ref)
        acc_ref[...] = jnp.zeros_like(acc_ref)
        update(0,
               [kpad_ref[:, h * D_HEAD:(h + 1) * D_HEAD].astype(BF16) for h in range(H_DIFF)],
               [vpad_ref[:, h * D_HEAD:(h + 1) * D_HEAD].astype(BF16) for h in range(H_DIFF)],
               alibi(PAGE, causal=True), jnp.zeros((n_rows, 1), F32))

    def head_rows(ref, h):
        return ref[pl.ds(h, PAGE, stride=H_DIFF), :].astype(BF16)

    rowi = lax.broadcasted_iota(jnp.int32, (n_rows, 1), 0)
    slope = jnp.exp2(-8.0 * ((rowi // rows_h).astype(F32) + 1.0) / H_DIFF)
    for g in range(n_grp):
        pages = range(g * ppg, (g + 1) * ppg)
        first_key = (j * n_pp + g * ppg) * PAGE
        shift = slope * (past - first_key).astype(F32)
        k_heads = [jnp.concatenate([head_rows(k_refs[i], h) for i in pages], axis=0) for h in range(H_DIFF)]
        v_heads = [jnp.concatenate([head_rows(v_refs[i], h) for i in pages], axis=0) for h in range(H_DIFF)]
        update(g, k_heads, v_heads, bias_ref[...], shift)

    @pl.when(j == pl.num_programs(1) - 1)
    def _():
        m_all = m_ref[0]
        for g in range(1, n_grp):
            m_all = jnp.maximum(m_all, m_ref[g])
        l_all = jnp.zeros_like(m_all)
        acc = jnp.zeros((n_rows, D_HEAD), F32)
        for g in range(n_grp):
            w = jnp.exp(m_ref[g] - m_all)
            l_all = l_all + l_ref[g] * w
            acc = acc + acc_ref[g] * w
        o_all = acc / l_all
        lam = _lambda(lq1_ref, lk1_ref, lq2_ref, lk2_ref, lam_init)
        for h in range(H_DIFF):
            sl = slice(h * D_HEAD, (h + 1) * D_HEAD)
            o = o_all[h * rows_h:h * rows_h + t, :] - lam * o_all[h * rows_h + t:(h + 1) * rows_h, :]
            y_ref[:, sl] = _diff_finish(o, gain_ref[h:h + 1, :], gate_ref[:, sl], lam_init).astype(y_ref.dtype)


def _diff_sample(page_table, qd, kd, vd, gd, gain, lams, cache_k, cache_v, n_batch, t, lam_init,
                 n_pp=16, n_grp=2, n_slots=3):
    n_pages = page_table.shape[1]
    past = n_pages * PAGE
    d_grp = H_DIFF * D_HEAD
    n_rows = H_DIFF * 2 * t
    n_steps = n_pages // n_pp
    assert n_pages % n_pp == 0 and n_batch * n_steps >= n_slots - 1
    pt_flat = page_table.reshape(-1)
    tokblk = pl.BlockSpec((t, d_grp), lambda b, j, pt: (b, 0))
    vec = pl.BlockSpec((1, DK_DIFF), lambda b, j, pt: (0, 0))
    page_cols = PAGE * H_DIFF
    n_pool = cache_k.shape[1]
    cache_k = cache_k.reshape(n_pool, page_cols, D_HEAD)
    cache_v = cache_v.reshape(n_pool, page_cols, D_HEAD)
    hbm = pl.BlockSpec(memory_space=pl.ANY)

    grid_spec = pltpu.PrefetchScalarGridSpec(
        num_scalar_prefetch=1,
        grid=(n_batch, n_steps),
        in_specs=[tokblk, tokblk, tokblk, tokblk,
                  pl.BlockSpec((H_DIFF, D_HEAD), lambda b, j, pt: (0, 0)),
                  vec, vec, vec, vec, hbm, hbm],
        out_specs=tokblk,
        scratch_shapes=[
            pltpu.VMEM((n_rows, D_HEAD), BF16),
            pltpu.VMEM((PAGE, d_grp), F32),
            pltpu.VMEM((PAGE, d_grp), F32),
            pltpu.VMEM((n_rows, (n_pp // n_grp) * PAGE), F32),
            pltpu.VMEM((n_grp, n_rows, 1), F32),
            pltpu.VMEM((n_grp, n_rows, 1), F32),
            pltpu.VMEM((n_grp, n_rows, D_HEAD), F32),
            pltpu.VMEM((n_slots, n_pp, page_cols, D_HEAD), F32),
            pltpu.VMEM((n_slots, n_pp, page_cols, D_HEAD), F32),
            pltpu.SemaphoreType.DMA((n_slots,)),
        ],
    )
    return pl.pallas_call(
        functools.partial(_diff_sample_kernel, t=t, n_pp=n_pp, n_grp=n_grp, n_slots=n_slots, past=past,
                          lam_init=lam_init),
        grid_spec=grid_spec,
        out_shape=jax.ShapeDtypeStruct((n_batch * t, d_grp), F32),
        compiler_params=_cparams(("arbitrary", "arbitrary")),
        name="diff_sample",
    )(pt_flat, qd, kd, vd, gd, gain, *lams, cache_k, cache_v)


def _finish_kernel(x_ref, yr_ref, yd_ref, w_ref, g_ref, o_ref):
    d_grp = yr_ref.shape[1]
    y = jnp.dot(yr_ref[...].astype(BF16), w_ref[:d_grp, :], preferred_element_type=F32)
    y = y + jnp.dot(yd_ref[...].astype(BF16), w_ref[d_grp:, :], preferred_element_type=F32)
    yn = y * lax.rsqrt(jnp.mean(y * y, axis=-1, keepdims=True) + NORM_EPS) * g_ref[...]
    o_ref[...] = x_ref[...] + yn


def _finish(x, y_ret, y_diff, w_bf16, norm_post, tm):
    n, d_model = x.shape
    d_grp = y_ret.shape[1]
    tok = lambda i: (i, 0)
    return pl.pallas_call(
        _finish_kernel,
        grid=(n // tm,),
        in_specs=[
            pl.BlockSpec((tm, d_model), tok),
            pl.BlockSpec((tm, d_grp), tok),
            pl.BlockSpec((tm, d_grp), tok),
            pl.BlockSpec((2 * d_grp, d_model), lambda i: (0, 0)),
            pl.BlockSpec((1, d_model), lambda i: (0, 0)),
        ],
        out_specs=pl.BlockSpec((tm, d_model), tok),
        out_shape=jax.ShapeDtypeStruct((n, d_model), F32),
        compiler_params=_cparams(("parallel",)),
        name="finish",
    )(x, y_ret, y_diff, w_bf16, norm_post.reshape(1, d_model))


def kernel(x_prompt, x_sample, cache_k, cache_v, page_table, state_ret, norm_pre, norm_post, w_in,
           ret_gn, diff_lq1, diff_lk1, diff_lq2, diff_lk2, diff_norm, w_out):
    batch, seq, d_model = x_prompt.shape
    n_dec, t_dec, _ = x_sample.shape
    depth = w_in.shape[0]
    assert depth == 1
    n_pages = page_table.shape[1]
    past = n_pages * PAGE
    layer = 0
    lam_init = 0.8 - 0.6 * math.exp(-0.3 * layer)

    lg = jnp.log(1.0 - jnp.exp2(-5.0 - jnp.arange(H_RET, dtype=F32)))
    assert 8 % H_DIFF == 0
    slope_exponents = -(8 // H_DIFF) * (jnp.arange(H_DIFF, dtype=jnp.int32) + 1)
    slopes = lax.bitcast_convert_type((slope_exponents + 127) << 23, F32)
    w_in16 = w_in[layer].astype(BF16)
    w_out16 = w_out[layer].astype(BF16)
    lams = [p[layer].reshape(1, DK_DIFF) for p in (diff_lq1, diff_lk1, diff_lq2, diff_lk2)]
    tm = 256

    xp = x_prompt.reshape(batch * seq, d_model)
    cos_p, sin_p = _rope_tables(jnp.arange(seq))
    qr, kr, vr, gr, qd, kd, vd, gd, kd16, vdt16 = _project(
        xp, norm_pre[layer], w_in16, cos_p, sin_p, tm, BF16, batch_seq=(batch, seq))
    y_ret, s_fin = _ret_prompt(lg, qr, kr, vr, gr, ret_gn[layer], batch, seq)
    y_diff = _diff_prompt(slopes, qd, kd16, vdt16, gd, diff_norm[layer], lams, batch, seq, lam_init)
    out_p = _finish(xp, y_ret, y_diff, w_out16, norm_post[layer], tm)

    xs = x_sample.reshape(n_dec * t_dec, d_model)
    cos_s, sin_s = _rope_tables(past + jnp.arange(tm) % t_dec)
    qr_s, kr_s, vr_s, gr_s, qd_s, kd_s, vd_s, gd_s, _, _ = _project(xs, norm_pre[layer], w_in16, cos_s, sin_s, tm, F32)
    y_ret_s, s_new = _ret_sample(lg, qr_s, kr_s, vr_s, gr_s, ret_gn[layer], state_ret[layer], n_dec, t_dec, n_seq=16)
    y_diff_s = _diff_sample(page_table, qd_s, kd_s, vd_s, gd_s, diff_norm[layer], lams, cache_k, cache_v,
                            n_dec, t_dec, lam_init)
    out_s = _finish(xs, y_ret_s, y_diff_s, w_out16, norm_post[layer], tm)

    return (
        out_p.reshape(batch, seq, d_model),
        out_s.reshape(n_dec, t_dec, d_model),
        kd,
        vd,
        s_fin.reshape(1, batch, H_RET, D_HEAD, D_HEAD),
        kd_s.reshape(1, n_dec, t_dec, H_DIFF, D_HEAD),
        vd_s.reshape(1, n_dec, t_dec, H_DIFF, D_HEAD),
        s_new.reshape(1, n_dec, H_RET, D_HEAD, D_HEAD),
    )
```

```python
import functools
import math

import jax
import jax.numpy as jnp
from jax import lax
from jax.experimental import pallas as pl
from jax.experimental.pallas import tpu as pltpu

F32 = jnp.float32
BF16 = jnp.bfloat16

H_RET = 4
H_DIFF = 4
D_HEAD = 128
DK_DIFF = 64
RET_CHUNK = 128
PAGE = 128
ROPE_BASE = 10000.0
NORM_EPS = 1e-6
GN_EPS = 1e-5
NEG_INF = float("-inf")

VMEM_LIMIT = 56 * 1024 * 1024


def _cparams(sem):
    return pltpu.CompilerParams(dimension_semantics=sem, vmem_limit_bytes=VMEM_LIMIT)


def _silu(x):
    return x * (1.0 / (1.0 + jnp.exp(-x)))


def _nt(a, b):
    return lax.dot_general(a, b, (((1,), (1,)), ((), ())), preferred_element_type=F32)


def _tn(a, b):
    return lax.dot_general(a, b, (((0,), (0,)), ((), ())), preferred_element_type=F32)


def _proj_kernel(x_ref, g_ref, w_ref, cos_ref, sin_ref,
                 qr_ref, kr_ref, vr_ref, gr_ref, qd_ref, kd_ref, vd_ref, gd_ref, kd16_ref, vdt16_ref):
    d_grp = H_RET * D_HEAD
    x = x_ref[...]
    ms = jnp.mean(x * x, axis=-1, keepdims=True)
    h = (x * lax.rsqrt(ms + NORM_EPS) * g_ref[...]).astype(BF16)
    cos2 = cos_ref[...]
    sin2 = sin_ref[...]

    def section(s):
        return jnp.dot(h, w_ref[:, s * d_grp:(s + 1) * d_grp], preferred_element_type=F32)

    def rope(z, scale):
        for hd in range(H_RET):
            zh = z[:, hd * D_HEAD:(hd + 1) * D_HEAD]
            rot = pltpu.roll(zh, D_HEAD // 2, axis=1)
            yield hd, (zh * cos2 + rot * sin2) * scale

    def store_heads(ref, z):
        if len(ref.shape) == 2:
            ref[...] = z
        else:
            for hd in range(H_DIFF):
                ref[:, hd, :] = z[:, hd * D_HEAD:(hd + 1) * D_HEAD]

    z = section(0)
    for hd, r in rope(z, 1.0):
        qr_ref[:, hd * D_HEAD:(hd + 1) * D_HEAD] = r.astype(qr_ref.dtype)
    z = section(1)
    for hd, r in rope(z, D_HEAD ** -0.5):
        kr_ref[:, hd * D_HEAD:(hd + 1) * D_HEAD] = r.astype(kr_ref.dtype)
    vr_ref[...] = section(2).astype(vr_ref.dtype)
    gr_ref[...] = section(3)
    qd_ref[...] = (section(4) * (DK_DIFF ** -0.5)).astype(qd_ref.dtype)
    z = section(5)
    store_heads(kd_ref, z)
    kd16_ref[...] = z.astype(BF16)
    z = section(6)
    store_heads(vd_ref, z)
    vdt16_ref[...] = z.T.astype(BF16)
    gd_ref[...] = section(7)


def _project(x, norm_pre, w_bf16, cos2, sin2, tm, ret_dtype, batch_seq=None):
    n, d_model = x.shape
    d_in = w_bf16.shape[1]
    d_grp = d_in // 8
    n_tab = cos2.shape[0] // tm
    tok = lambda i: (i, 0)
    out_blk = pl.BlockSpec((tm, d_grp), tok)
    if batch_seq is None:
        kv_shape = jax.ShapeDtypeStruct((n, d_grp), F32)
        kv_blk = out_blk
    else:
        batch, seq = batch_seq
        per_seq = seq // tm
        kv_shape = jax.ShapeDtypeStruct((1, batch, seq, H_DIFF, D_HEAD), F32)
        kv_blk = pl.BlockSpec((None, None, tm, H_DIFF, D_HEAD), lambda i: (0, i // per_seq, i % per_seq, 0, 0))
    shapes = [
        jax.ShapeDtypeStruct((n, d_grp), ret_dtype),
        jax.ShapeDtypeStruct((n, d_grp), ret_dtype),
        jax.ShapeDtypeStruct((n, d_grp), ret_dtype),
        jax.ShapeDtypeStruct((n, d_grp), F32),
        jax.ShapeDtypeStruct((n, d_grp), ret_dtype),
        kv_shape,
        kv_shape,
        jax.ShapeDtypeStruct((n, d_grp), F32),
        jax.ShapeDtypeStruct((n, d_grp), BF16),
        jax.ShapeDtypeStruct((d_grp, n), BF16),
    ]
    out_specs = [out_blk] * 5 + [kv_blk, kv_blk, out_blk, out_blk, pl.BlockSpec((d_grp, tm), lambda i: (0, i))]
    return pl.pallas_call(
        _proj_kernel,
        grid=(n // tm,),
        in_specs=[
            pl.BlockSpec((tm, d_model), tok),
            pl.BlockSpec((1, d_model), lambda i: (0, 0)),
            pl.BlockSpec((d_model, d_in), lambda i: (0, 0)),
            pl.BlockSpec((tm, D_HEAD), lambda i: (i % n_tab, 0)),
            pl.BlockSpec((tm, D_HEAD), lambda i: (i % n_tab, 0)),
        ],
        out_specs=out_specs,
        out_shape=shapes,
        compiler_params=_cparams(("parallel",)),
        name="proj",
    )(x, norm_pre.reshape(1, d_model), w_bf16, cos2, sin2)


def _rope_tables(pos):
    half = D_HEAD // 2
    inv = ROPE_BASE ** (-jnp.arange(half, dtype=F32) / half)
    ang = pos.astype(F32)[:, None] * inv[None, :]
    cos, sin = jnp.cos(ang), jnp.sin(ang)
    return jnp.concatenate([cos, cos], axis=-1), jnp.concatenate([-sin, sin], axis=-1)


def _group_norm_gate(o, gain, gate):
    mu = jnp.mean(o, axis=-1, keepdims=True)
    d = o - mu
    var = jnp.mean(d * d, axis=-1, keepdims=True)
    return d * lax.rsqrt(var + GN_EPS) * gain * _silu(gate)


def _decay_terms(n_rows, chunk, lg):
    row = lax.broadcasted_iota(jnp.int32, (n_rows, n_rows), 0)
    col = lax.broadcasted_iota(jnp.int32, (n_rows, n_rows), 1)
    dist = (row - col).astype(F32)
    dmask = jnp.where(col <= row, jnp.exp(jnp.maximum(dist, 0.0) * lg), 0.0)
    if n_rows != chunk:
        dmask = jnp.where(row // chunk == col // chunk, dmask, 0.0)
    idx = (lax.broadcasted_iota(jnp.int32, (n_rows, 1), 0) % chunk).astype(F32)
    xi = jnp.exp((idx + 1.0) * lg)
    zeta = jnp.exp((chunk - 1.0 - idx) * lg)
    g_chunk = jnp.exp(jnp.full((1, D_HEAD), chunk, F32) * lg)
    return dmask, xi, zeta, g_chunk


def _ret_prompt_kernel(lg_ref, q_ref, k_ref, v_ref, gate_ref, gn_ref, y_ref, s_out_ref, s_scr):
    c = pl.program_id(1)

    @pl.when(c == 0)
    def _():
        s_scr[...] = jnp.zeros_like(s_scr)

    for h in range(H_RET):
        sl = slice(h * D_HEAD, (h + 1) * D_HEAD)
        lg = lg_ref[h]
        dmask, xi, zeta, g_chunk = _decay_terms(RET_CHUNK, RET_CHUNK, lg)
        q = q_ref[:, sl]
        k = k_ref[:, sl]
        v = v_ref[:, sl]
        s_old = s_scr[h]
        inner = _nt(q, k) * dmask
        o = jnp.dot(inner.astype(BF16), v, preferred_element_type=F32)
        o = o + jnp.dot(q, s_old.astype(BF16), preferred_element_type=F32) * xi
        kz = (k.astype(F32) * zeta).astype(BF16)
        s_scr[h] = s_old * g_chunk + _tn(kz, v)
        y_ref[:, sl] = _group_norm_gate(o, gn_ref[h:h + 1, :], gate_ref[:, sl]).astype(y_ref.dtype)

    @pl.when(c == pl.num_programs(1) - 1)
    def _():
        s_out_ref[...] = s_scr[...]


def _ret_prompt(lg, qr, kr, vr, gr, gn, batch, seq):
    n_chunk = seq // RET_CHUNK
    d_grp = H_RET * D_HEAD
    blk = pl.BlockSpec((RET_CHUNK, d_grp), lambda b, c: (b * n_chunk + c, 0))
    return pl.pallas_call(
        _ret_prompt_kernel,
        grid=(batch, n_chunk),
        in_specs=[
            pl.BlockSpec(memory_space=pltpu.SMEM),
            blk, blk, blk, blk,
            pl.BlockSpec((H_RET, D_HEAD), lambda b, c: (0, 0)),
        ],
        out_specs=[
            blk,
            pl.BlockSpec((None, H_RET, D_HEAD, D_HEAD), lambda b, c: (b, 0, 0, 0)),
        ],
        out_shape=[
            jax.ShapeDtypeStruct((batch * seq, d_grp), BF16),
            jax.ShapeDtypeStruct((batch, H_RET, D_HEAD, D_HEAD), F32),
        ],
        scratch_shapes=[pltpu.VMEM((H_RET, D_HEAD, D_HEAD), F32)],
        compiler_params=_cparams(("parallel", "arbitrary")),
        name="ret_prompt",
    )(lg, qr, kr, vr, gr, gn)


def _ret_sample_kernel(lg_ref, q_ref, k_ref, v_ref, gate_ref, gn_ref, s_ref, y_ref, s_out_ref, *, n_seq, t):
    n_rows = n_seq * t
    seq_of_row = lax.broadcasted_iota(jnp.int32, (n_rows, 1), 0) // t
    for h in range(H_RET):
        sl = slice(h * D_HEAD, (h + 1) * D_HEAD)
        lg = lg_ref[h]
        dmask, xi, zeta, g_chunk = _decay_terms(n_rows, t, lg)
        qf = q_ref[:, sl]
        q = qf.astype(BF16)
        v = v_ref[:, sl].astype(BF16)
        kf = k_ref[:, sl]
        inner = _nt(q, kf.astype(BF16)) * dmask
        o = jnp.dot(inner.astype(BF16), v, preferred_element_type=F32)
        kz = kf * zeta
        cross = []
        for b in range(n_seq):
            s_old = s_ref[b, h]
            q_b = qf[b * t:(b + 1) * t, :].astype(BF16)
            cross.append(jnp.dot(q_b, s_old.astype(BF16), preferred_element_type=F32))
            kz_b = jnp.where(seq_of_row == b, kz, 0.0).astype(BF16)
            s_out_ref[b, h] = s_old * g_chunk + _tn(kz_b, v)
        o = o + jnp.concatenate(cross, axis=0) * xi
        y_ref[:, sl] = _group_norm_gate(o, gn_ref[h:h + 1, :], gate_ref[:, sl]).astype(y_ref.dtype)


def _ret_sample(lg, qr, kr, vr, gr, gn, state, n_batch, t, n_seq):
    d_grp = H_RET * D_HEAD
    blk = pl.BlockSpec((n_seq * t, d_grp), lambda i: (i, 0))
    sblk = pl.BlockSpec((n_seq, H_RET, D_HEAD, D_HEAD), lambda i: (i, 0, 0, 0))
    return pl.pallas_call(
        functools.partial(_ret_sample_kernel, n_seq=n_seq, t=t),
        grid=(n_batch // n_seq,),
        in_specs=[
            pl.BlockSpec(memory_space=pltpu.SMEM),
            blk, blk, blk, blk,
            pl.BlockSpec((H_RET, D_HEAD), lambda i: (0, 0)),
            sblk,
        ],
        out_specs=[blk, sblk],
        out_shape=[
            jax.ShapeDtypeStruct((n_batch * t, d_grp), BF16),
            jax.ShapeDtypeStruct((n_batch, H_RET, D_HEAD, D_HEAD), F32),
        ],
        compiler_params=_cparams(("parallel",)),
        name="ret_sample",
    )(lg, qr, kr, vr, gr, gn, state)


def _lambda(lq1_ref, lk1_ref, lq2_ref, lk2_ref, lam_init):
    a = jnp.sum(lq1_ref[...] * lk1_ref[...], axis=-1, keepdims=True)
    b = jnp.sum(lq2_ref[...] * lk2_ref[...], axis=-1, keepdims=True)
    return jnp.exp(a) - jnp.exp(b) + lam_init


def _diff_finish(o, gain, gate, lam_init):
    y = o * lax.rsqrt(jnp.mean(o * o, axis=-1, keepdims=True) + NORM_EPS)
    return y * gain * (1.0 - lam_init) * _silu(gate)


def _map_split(q):
    lane = lax.broadcasted_iota(jnp.int32, q.shape, 1)
    zero = jnp.zeros_like(q)
    return jnp.concatenate([jnp.where(lane < DK_DIFF, q, zero), jnp.where(lane >= DK_DIFF, q, zero)], axis=0)


def _diff_prompt_tile(h, qi, lam, slope_ref, q_ref, k_ref, vt_ref, gate_ref, gain_ref, y_ref,
                      m_ref, l_ref, acc_ref, qbd_ref, sa_ref, sb_ref, *, tq, tk, cw, lam_init):
    slope = slope_ref[h]
    lane_q = lax.broadcasted_iota(jnp.int32, (2 * tq, D_HEAD), 1)
    qbd_ref[:, :D_HEAD] = _map_split(q_ref[...])
    qbd_ref[:, D_HEAD:] = jnp.where(lane_q < 2, 1.0, 0.0).astype(BF16)
    row_k = lax.broadcasted_iota(jnp.int32, (tk, D_HEAD), 0)
    lane_k = lax.broadcasted_iota(jnp.int32, (tk, D_HEAD), 1)
    j_part = jnp.where(lane_k == 0, (row_k // 16) * 16, jnp.where(lane_k == 1, row_k % 16, 0))
    k_aux = (j_part.astype(F32) * slope).astype(BF16)
    kml = lax.broadcasted_iota(jnp.int32, (tk, cw), 0) - lax.broadcasted_iota(jnp.int32, (tk, cw), 1)

    m_ref[...] = jnp.full_like(m_ref, NEG_INF)
    l_ref[...] = jnp.zeros_like(l_ref)
    acc_ref[...] = jnp.zeros_like(acc_ref)

    def scores(ki, s_ref):
        start = pl.multiple_of(ki * tk, tk)
        k_aug = jnp.concatenate([k_ref[pl.ds(start, tk), :], k_aux], axis=1)
        s_ref[...] = _nt(k_aug, qbd_ref[...])

    def absorb(ki, s_ref, diag_off):
        start = pl.multiple_of(ki * tk, tk)
        vt = vt_ref[:, pl.ds(start, tk)]
        for c in range(2 * tq // cw):
            cols = slice(c * cw, (c + 1) * cw)
            q0 = (c * cw) % tq
            if diag_off is not None and diag_off >= q0 + cw:
                continue
            s = s_ref[:, cols]
            if diag_off is not None:
                s = jnp.where(kml <= q0 - diag_off, s, NEG_INF)
            off = slope * (qi * tq - ki * tk).astype(F32)
            m = m_ref[:, cols]
            m_new = jnp.maximum(m, jnp.max(s, axis=0, keepdims=True) - off)
            corr = jnp.exp(m - m_new)
            p = jnp.exp(s - (m_new + off))
            l_ref[:, cols] = l_ref[:, cols] * corr + jnp.sum(p, axis=0, keepdims=True)
            acc_ref[:, cols] = acc_ref[:, cols] * corr + jnp.dot(vt, p.astype(BF16), preferred_element_type=F32)
            m_ref[:, cols] = m_new

    assert tq == 2 * tk
    n_full = 2 * qi
    scores(0, sa_ref)

    def body(pi, carry):
        scores(2 * pi + 1, sb_ref)
        absorb(2 * pi, sa_ref, None)
        scores(2 * pi + 2, sa_ref)
        absorb(2 * pi + 1, sb_ref, None)
        return carry

    lax.fori_loop(0, qi, body, 0)
    scores(n_full + 1, sb_ref)
    absorb(n_full, sa_ref, 0)
    absorb(n_full + 1, sb_ref, tk)

    o_t = acc_ref[...] / l_ref[...]
    o = (o_t[:, :tq] - lam * o_t[:, tq:]).T
    y_ref[...] = _diff_finish(o, gain_ref[pl.ds(h, 1), :], gate_ref[...], lam_init).astype(y_ref.dtype)


def _diff_sample_sequence(seq, lam, pt_ref, q_ref, kl_ref, vl_ref, gate_ref, gain_ref, ck_hbm, cv_hbm, y_ref,
                          qall_ref, kpad_ref, vpad_ref, bias_ref, m_ref, l_ref, acc_ref, kbuf, vbuf, sem,
                          *, t, n_seq, n_pp, n_grp, n_slots, n_steps, past, lam_init):
    n_total = n_seq * n_steps
    rows_h = 2 * t
    n_rows = H_DIFF * rows_h
    ppg = n_pp // n_grp

    def batch_copies(src_step, slot, known_pages):
        for i in range(n_pp):
            page = pt_ref[src_step * n_pp + i] if known_pages else 0
            yield pltpu.make_async_copy(ck_hbm.at[page], kbuf.at[slot, i], sem.at[slot])
            yield pltpu.make_async_copy(cv_hbm.at[page], vbuf.at[slot, i], sem.at[slot])

    def start_batch(src_step):
        for cp in batch_copies(src_step, src_step % n_slots, True):
            cp.start()

    @pl.when(seq == 0)
    def _():
        for d in range(n_slots - 1):
            start_batch(d)

    def alibi(width, causal):
        rowi = lax.broadcasted_iota(jnp.int32, (n_rows, width), 0)
        tok_k = lax.broadcasted_iota(jnp.int32, (n_rows, width), 1)
        slope = jnp.exp2(-8.0 * ((rowi // rows_h).astype(F32) + 1.0) / H_DIFF)
        tok_q = rowi % t
        bias = -slope * (tok_q - tok_k).astype(F32)
        if causal:
            bias = jnp.where(tok_k <= tok_q, bias, NEG_INF)
        return bias

    def update(g, k_heads, v_heads, bias, shift):
        sa = jnp.concatenate(
            [_nt(qall_ref[h * rows_h:(h + 1) * rows_h, :], k_heads[h]) for h in range(H_DIFF)], axis=0) + bias
        m = m_ref[g]
        m_new = jnp.maximum(m, jnp.max(sa, axis=-1, keepdims=True) - shift)
        corr = jnp.exp(m - m_new)
        p = jnp.exp(sa - (m_new + shift))
        l_ref[g] = l_ref[g] * corr + jnp.sum(p, axis=-1, keepdims=True)
        pv = jnp.concatenate(
            [jnp.dot(p[h * rows_h:(h + 1) * rows_h, :].astype(BF16), v_heads[h], preferred_element_type=F32)
             for h in range(H_DIFF)], axis=0)
        acc_ref[g] = acc_ref[g] * corr + pv
        m_ref[g] = m_new

    qall_ref[...] = jnp.concatenate(
        [_map_split(q_ref[:, h * D_HEAD:(h + 1) * D_HEAD]) for h in range(H_DIFF)], axis=0).astype(BF16)
    kpad_ref[...] = jnp.zeros_like(kpad_ref)
    vpad_ref[...] = jnp.zeros_like(vpad_ref)
    kpad_ref[0:t, :] = kl_ref[...]
    vpad_ref[0:t, :] = vl_ref[...]
    bias_ref[...] = alibi(ppg * PAGE, causal=False)
    m_ref[...] = jnp.full_like(m_ref, NEG_INF)
    l_ref[...] = jnp.zeros_like(l_ref)
    acc_ref[...] = jnp.zeros_like(acc_ref)
    update(0,
           [kpad_ref[:, h * D_HEAD:(h + 1) * D_HEAD].astype(BF16) for h in range(H_DIFF)],
           [vpad_ref[:, h * D_HEAD:(h + 1) * D_HEAD].astype(BF16) for h in range(H_DIFF)],
           alibi(PAGE, causal=True), jnp.zeros((n_rows, 1), F32))

    def head_rows(ref, h):
        return ref[pl.ds(h, PAGE, stride=H_DIFF), :].astype(BF16)

    rowi = lax.broadcasted_iota(jnp.int32, (n_rows, 1), 0)
    slope = jnp.exp2(-8.0 * ((rowi // rows_h).astype(F32) + 1.0) / H_DIFF)
    for j in range(n_steps):
        step = seq * n_steps + j

        @pl.when(step + (n_slots - 1) < n_total)
        def _():
            start_batch(step + (n_slots - 1))

        slot = step % n_slots
        for cp in batch_copies(step, slot, False):
            cp.wait()
        for g in range(n_grp):
            pages = range(g * ppg, (g + 1) * ppg)
            first_key = (j * n_pp + g * ppg) * PAGE
            shift = slope * float(past - first_key)
            k_heads = [jnp.concatenate([head_rows(kbuf.at[slot, i], h) for i in pages], axis=0)
                       for h in range(H_DIFF)]
            v_heads = [jnp.concatenate([head_rows(vbuf.at[slot, i], h) for i in pages], axis=0)
                       for h in range(H_DIFF)]
            update(g, k_heads, v_heads, bias_ref[...], shift)

    m_all = m_ref[0]
    for g in range(1, n_grp):
        m_all = jnp.maximum(m_all, m_ref[g])
    l_all = jnp.zeros_like(m_all)
    acc = jnp.zeros((n_rows, D_HEAD), F32)
    for g in range(n_grp):
        w = jnp.exp(m_ref[g] - m_all)
        l_all = l_all + l_ref[g] * w
        acc = acc + acc_ref[g] * w
    o_all = acc / l_all
    for h in range(H_DIFF):
        sl = slice(h * D_HEAD, (h + 1) * D_HEAD)
        o = o_all[h * rows_h:h * rows_h + t, :] - lam * o_all[h * rows_h + t:(h + 1) * rows_h, :]
        y_ref[:, sl] = _diff_finish(o, gain_ref[h:h + 1, :], gate_ref[:, sl], lam_init).astype(y_ref.dtype)


def _diff_attention_kernel(pt_ref, slope_ref, qp_ref, kp_ref, vtp_ref, gatep_ref,
                           qs_ref, kls_ref, vls_ref, gates_ref, gain_ref,
                           lq1_ref, lk1_ref, lq2_ref, lk2_ref, ck_hbm, cv_hbm, yp_ref, ys_ref,
                           m_ref, l_ref, acc_ref, qbd_ref, sa_ref, sb_ref,
                           qall_ref, kpad_ref, vpad_ref, bias_ref, ms_ref, ls_ref, accs_ref, kbuf, vbuf, sem,
                           *, nq, prompt_kw, sample_kw):
    g = pl.program_id(0)
    lam = _lambda(lq1_ref, lk1_ref, lq2_ref, lk2_ref, prompt_kw["lam_init"])
    _diff_sample_sequence(g, lam, pt_ref, qs_ref, kls_ref, vls_ref, gates_ref, gain_ref,
                          ck_hbm, cv_hbm, ys_ref, qall_ref, kpad_ref, vpad_ref, bias_ref, ms_ref, ls_ref,
                          accs_ref, kbuf, vbuf, sem, **sample_kw)
    h = (g // nq) % H_DIFF
    _diff_prompt_tile(h, _tile_order(g % nq, nq), lam, slope_ref, qp_ref, kp_ref, vtp_ref, gatep_ref, gain_ref,
                      yp_ref, m_ref, l_ref, acc_ref, qbd_ref, sa_ref, sb_ref, **prompt_kw)


def _tile_order(i, nq):
    return jnp.where(i % 2 == 0, nq - 1 - i // 2, i // 2)


def _diff_attention(slopes, qd, kd16, vdt16, gd, qd_s, kd_s, vd_s, gd_s, gain, lams, page_table, cache_k, cache_v,
                    batch, seq, n_dec, t, lam_init, tq=512, tk=256, cw=256, n_pp=16, n_grp=2, n_slots=4):
    assert tq % cw == 0
    nq = seq // tq
    n_pages = page_table.shape[1]
    past = n_pages * PAGE
    d_grp = H_DIFF * D_HEAD
    n_rows = H_DIFF * 2 * t
    n_steps = n_pages // n_pp
    assert n_dec == batch * H_DIFF * nq and n_pages % n_pp == 0 and n_dec * n_steps >= n_slots - 1
    pt_flat = page_table.reshape(-1)
    page_cols = PAGE * H_DIFF
    n_pool = cache_k.shape[1]
    cache_k = cache_k.reshape(n_pool, page_cols, D_HEAD)
    cache_v = cache_v.reshape(n_pool, page_cols, D_HEAD)

    def tile(g):
        return g // (H_DIFF * nq), (g // nq) % H_DIFF, _tile_order(g % nq, nq)

    def q_map(g, pt):
        b, h, qi = tile(g)
        return b * nq + qi, h

    qblk = pl.BlockSpec((tq, D_HEAD), q_map)
    tokblk = pl.BlockSpec((t, d_grp), lambda g, pt: (g, 0))
    vec = pl.BlockSpec((1, DK_DIFF), lambda g, pt: (0, 0))
    hbm = pl.BlockSpec(memory_space=pl.ANY)
    grid_spec = pltpu.PrefetchScalarGridSpec(
        num_scalar_prefetch=1,
        grid=(n_dec,),
        in_specs=[
            pl.BlockSpec(memory_space=pltpu.SMEM),
            qblk,
            pl.BlockSpec((seq, D_HEAD), lambda g, pt: (tile(g)[0], tile(g)[1])),
            pl.BlockSpec((D_HEAD, seq), lambda g, pt: (tile(g)[1], tile(g)[0])),
            qblk,
            tokblk, tokblk, tokblk, tokblk,
            pl.BlockSpec((H_DIFF, D_HEAD), lambda g, pt: (0, 0)),
            vec, vec, vec, vec, hbm, hbm,
        ],
        out_specs=[qblk, tokblk],
        scratch_shapes=[
            pltpu.VMEM((1, 2 * tq), F32),
            pltpu.VMEM((1, 2 * tq), F32),
            pltpu.VMEM((D_HEAD, 2 * tq), F32),
            pltpu.VMEM((2 * tq, 2 * D_HEAD), BF16),
            pltpu.VMEM((tk, 2 * tq), F32),
            pltpu.VMEM((tk, 2 * tq), F32),
            pltpu.VMEM((n_rows, D_HEAD), BF16),
            pltpu.VMEM((PAGE, d_grp), F32),
            pltpu.VMEM((PAGE, d_grp), F32),
            pltpu.VMEM((n_rows, (n_pp // n_grp) * PAGE), F32),
            pltpu.VMEM((n_grp, n_rows, 1), F32),
            pltpu.VMEM((n_grp, n_rows, 1), F32),
            pltpu.VMEM((n_grp, n_rows, D_HEAD), F32),
            pltpu.VMEM((n_slots, n_pp, page_cols, D_HEAD), F32),
            pltpu.VMEM((n_slots, n_pp, page_cols, D_HEAD), F32),
            pltpu.SemaphoreType.DMA((n_slots,)),
        ],
    )
    return pl.pallas_call(
        functools.partial(
            _diff_attention_kernel, nq=nq,
            prompt_kw=dict(tq=tq, tk=tk, cw=cw, lam_init=lam_init),
            sample_kw=dict(t=t, n_seq=n_dec, n_pp=n_pp, n_grp=n_grp, n_slots=n_slots, n_steps=n_steps, past=past,
                           lam_init=lam_init)),
        grid_spec=grid_spec,
        out_shape=[jax.ShapeDtypeStruct((batch * seq, d_grp), BF16),
                   jax.ShapeDtypeStruct((n_dec * t, d_grp), F32)],
        compiler_params=_cparams(("arbitrary",)),
        name="diff_attention",
    )(pt_flat, slopes, qd, kd16, vdt16, gd, qd_s, kd_s, vd_s, gd_s, gain, *lams, cache_k, cache_v)


def _finish_kernel(x_ref, yr_ref, yd_ref, w_ref, g_ref, o_ref):
    d_grp = yr_ref.shape[1]
    y = jnp.dot(yr_ref[...].astype(BF16), w_ref[:d_grp, :], preferred_element_type=F32)
    y = y + jnp.dot(yd_ref[...].astype(BF16), w_ref[d_grp:, :], preferred_element_type=F32)
    yn = y * lax.rsqrt(jnp.mean(y * y, axis=-1, keepdims=True) + NORM_EPS) * g_ref[...]
    o_ref[...] = x_ref[...] + yn


def _finish(x, y_ret, y_diff, w_bf16, norm_post, tm):
    n, d_model = x.shape
    d_grp = y_ret.shape[1]
    tok = lambda i: (i, 0)
    return pl.pallas_call(
        _finish_kernel,
        grid=(n // tm,),
        in_specs=[
            pl.BlockSpec((tm, d_model), tok),
            pl.BlockSpec((tm, d_grp), tok),
            pl.BlockSpec((tm, d_grp), tok),
            pl.BlockSpec((2 * d_grp, d_model), lambda i: (0, 0)),
            pl.BlockSpec((1, d_model), lambda i: (0, 0)),
        ],
        out_specs=pl.BlockSpec((tm, d_model), tok),
        out_shape=jax.ShapeDtypeStruct((n, d_model), F32),
        compiler_params=_cparams(("parallel",)),
        name="finish",
    )(x, y_ret, y_diff, w_bf16, norm_post.reshape(1, d_model))


def kernel(x_prompt, x_sample, cache_k, cache_v, page_table, state_ret, norm_pre, norm_post, w_in,
           ret_gn, diff_lq1, diff_lk1, diff_lq2, diff_lk2, diff_norm, w_out):
    batch, seq, d_model = x_prompt.shape
    n_dec, t_dec, _ = x_sample.shape
    depth = w_in.shape[0]
    assert depth == 1
    n_pages = page_table.shape[1]
    past = n_pages * PAGE
    layer = 0
    lam_init = 0.8 - 0.6 * math.exp(-0.3 * layer)

    lg = jnp.log(1.0 - jnp.exp2(-5.0 - jnp.arange(H_RET, dtype=F32)))
    assert 8 % H_DIFF == 0
    slope_exponents = -(8 // H_DIFF) * (jnp.arange(H_DIFF, dtype=jnp.int32) + 1)
    slopes = lax.bitcast_convert_type((slope_exponents + 127) << 23, F32)
    w_in16 = w_in[layer].astype(BF16)
    w_out16 = w_out[layer].astype(BF16)
    lams = [p[layer].reshape(1, DK_DIFF) for p in (diff_lq1, diff_lk1, diff_lq2, diff_lk2)]
    tm = 256

    xp = x_prompt.reshape(batch * seq, d_model)
    cos_p, sin_p = _rope_tables(jnp.arange(seq))
    qr, kr, vr, gr, qd, kd, vd, gd, kd16, vdt16 = _project(
        xp, norm_pre[layer], w_in16, cos_p, sin_p, tm, BF16, batch_seq=(batch, seq))
    xs = x_sample.reshape(n_dec * t_dec, d_model)
    cos_s, sin_s = _rope_tables(past + jnp.arange(tm) % t_dec)
    qr_s, kr_s, vr_s, gr_s, qd_s, kd_s, vd_s, gd_s, _, _ = _project(xs, norm_pre[layer], w_in16, cos_s, sin_s, tm, F32)

    y_ret, s_fin = _ret_prompt(lg, qr, kr, vr, gr, ret_gn[layer], batch, seq)
    y_ret_s, s_new = _ret_sample(lg, qr_s, kr_s, vr_s, gr_s, ret_gn[layer], state_ret[layer], n_dec, t_dec, n_seq=16)
    y_diff, y_diff_s = _diff_attention(slopes, qd, kd16, vdt16, gd, qd_s, kd_s, vd_s, gd_s, diff_norm[layer], lams,
                                       page_table, cache_k, cache_v, batch, seq, n_dec, t_dec, lam_init)
    out_p = _finish(xp, y_ret, y_diff, w_out16, norm_post[layer], tm)
    out_s = _finish(xs, y_ret_s, y_diff_s, w_out16, norm_post[layer], tm)

    return (
        out_p.reshape(batch, seq, d_model),
        out_s.reshape(n_dec, t_dec, d_model),
        kd,
        vd,
        s_fin.reshape(1, batch, H_RET, D_HEAD, D_HEAD),
        kd_s.reshape(1, n_dec, t_dec, H_DIFF, D_HEAD),
        vd_s.reshape(1, n_dec, t_dec, H_DIFF, D_HEAD),
        s_new.reshape(1, n_dec, H_RET, D_HEAD, D_HEAD),
    )
```

```python
import functools
import math

import jax
import jax.numpy as jnp
from jax import lax
from jax.experimental import pallas as pl
from jax.experimental.pallas import tpu as pltpu

F32 = jnp.float32
BF16 = jnp.bfloat16

H_RET = 4
H_DIFF = 4
D_HEAD = 128
DK_DIFF = 64
RET_CHUNK = 128
PAGE = 128
ROPE_BASE = 10000.0
NORM_EPS = 1e-6
GN_EPS = 1e-5
NEG_INF = float("-inf")

VMEM_LIMIT = 56 * 1024 * 1024


def _cparams(sem):
    return pltpu.CompilerParams(dimension_semantics=sem, vmem_limit_bytes=VMEM_LIMIT)


def _silu(x):
    return x * (1.0 / (1.0 + jnp.exp(-x)))


def _nt(a, b):
    return lax.dot_general(a, b, (((1,), (1,)), ((), ())), preferred_element_type=F32)


def _tn(a, b):
    return lax.dot_general(a, b, (((0,), (0,)), ((), ())), preferred_element_type=F32)


def _proj_kernel(x_ref, g_ref, w_ref, cos_ref, sin_ref,
                 qr_ref, kr_ref, vr_ref, gr_ref, qd_ref, kd_ref, vd_ref, gd_ref, kd16_ref, vdt16_ref):
    d_grp = H_RET * D_HEAD
    x = x_ref[...]
    ms = jnp.mean(x * x, axis=-1, keepdims=True)
    h = (x * lax.rsqrt(ms + NORM_EPS) * g_ref[...]).astype(BF16)
    cos2 = cos_ref[...]
    sin2 = sin_ref[...]

    def section(s):
        return jnp.dot(h, w_ref[:, s * d_grp:(s + 1) * d_grp], preferred_element_type=F32)

    def rope(z, scale):
        for hd in range(H_RET):
            zh = z[:, hd * D_HEAD:(hd + 1) * D_HEAD]
            rot = pltpu.roll(zh, D_HEAD // 2, axis=1)
            yield hd, (zh * cos2 + rot * sin2) * scale

    def store_heads(ref, z):
        if len(ref.shape) == 2:
            ref[...] = z
        else:
            for hd in range(H_DIFF):
                ref[:, hd, :] = z[:, hd * D_HEAD:(hd + 1) * D_HEAD]

    z = section(0)
    for hd, r in rope(z, 1.0):
        qr_ref[:, hd * D_HEAD:(hd + 1) * D_HEAD] = r.astype(qr_ref.dtype)
    z = section(1)
    for hd, r in rope(z, D_HEAD ** -0.5):
        kr_ref[:, hd * D_HEAD:(hd + 1) * D_HEAD] = r.astype(kr_ref.dtype)
    vr_ref[...] = section(2).astype(vr_ref.dtype)
    gr_ref[...] = section(3)
    qd_ref[...] = (section(4) * (DK_DIFF ** -0.5)).astype(qd_ref.dtype)
    z = section(5)
    store_heads(kd_ref, z)
    kd16_ref[...] = z.astype(BF16)
    z = section(6)
    store_heads(vd_ref, z)
    vdt16_ref[...] = z.T.astype(BF16)
    gd_ref[...] = section(7)


def _project(x, norm_pre, w_bf16, cos2, sin2, tm, ret_dtype, batch_seq=None):
    n, d_model = x.shape
    d_in = w_bf16.shape[1]
    d_grp = d_in // 8
    n_tab = cos2.shape[0] // tm
    tok = lambda i: (i, 0)
    out_blk = pl.BlockSpec((tm, d_grp), tok)
    if batch_seq is None:
        kv_shape = jax.ShapeDtypeStruct((n, d_grp), F32)
        kv_blk = out_blk
    else:
        batch, seq = batch_seq
        per_seq = seq // tm
        kv_shape = jax.ShapeDtypeStruct((1, batch, seq, H_DIFF, D_HEAD), F32)
        kv_blk = pl.BlockSpec((None, None, tm, H_DIFF, D_HEAD), lambda i: (0, i // per_seq, i % per_seq, 0, 0))
    shapes = [
        jax.ShapeDtypeStruct((n, d_grp), ret_dtype),
        jax.ShapeDtypeStruct((n, d_grp), ret_dtype),
        jax.ShapeDtypeStruct((n, d_grp), ret_dtype),
        jax.ShapeDtypeStruct((n, d_grp), F32),
        jax.ShapeDtypeStruct((n, d_grp), ret_dtype),
        kv_shape,
        kv_shape,
        jax.ShapeDtypeStruct((n, d_grp), F32),
        jax.ShapeDtypeStruct((n, d_grp), BF16),
        jax.ShapeDtypeStruct((d_grp, n), BF16),
    ]
    out_specs = [out_blk] * 5 + [kv_blk, kv_blk, out_blk, out_blk, pl.BlockSpec((d_grp, tm), lambda i: (0, i))]
    return pl.pallas_call(
        _proj_kernel,
        grid=(n // tm,),
        in_specs=[
            pl.BlockSpec((tm, d_model), tok),
            pl.BlockSpec((1, d_model), lambda i: (0, 0)),
            pl.BlockSpec((d_model, d_in), lambda i: (0, 0)),
            pl.BlockSpec((tm, D_HEAD), lambda i: (i % n_tab, 0)),
            pl.BlockSpec((tm, D_HEAD), lambda i: (i % n_tab, 0)),
        ],
        out_specs=out_specs,
        out_shape=shapes,
        compiler_params=_cparams(("parallel",)),
        name="proj",
    )(x, norm_pre.reshape(1, d_model), w_bf16, cos2, sin2)


def _rope_tables(pos):
    half = D_HEAD // 2
    inv = ROPE_BASE ** (-jnp.arange(half, dtype=F32) / half)
    ang = pos.astype(F32)[:, None] * inv[None, :]
    cos, sin = jnp.cos(ang), jnp.sin(ang)
    return jnp.concatenate([cos, cos], axis=-1), jnp.concatenate([-sin, sin], axis=-1)


def _group_norm_gate(o, gain, gate):
    mu = jnp.mean(o, axis=-1, keepdims=True)
    d = o - mu
    var = jnp.mean(d * d, axis=-1, keepdims=True)
    return d * lax.rsqrt(var + GN_EPS) * gain * _silu(gate)


def _decay_terms(n_rows, chunk, lg):
    row = lax.broadcasted_iota(jnp.int32, (n_rows, n_rows), 0)
    col = lax.broadcasted_iota(jnp.int32, (n_rows, n_rows), 1)
    dist = (row - col).astype(F32)
    dmask = jnp.where(col <= row, jnp.exp(jnp.maximum(dist, 0.0) * lg), 0.0)
    if n_rows != chunk:
        dmask = jnp.where(row // chunk == col // chunk, dmask, 0.0)
    idx = (lax.broadcasted_iota(jnp.int32, (n_rows, 1), 0) % chunk).astype(F32)
    xi = jnp.exp((idx + 1.0) * lg)
    zeta = jnp.exp((chunk - 1.0 - idx) * lg)
    g_chunk = jnp.exp(jnp.full((1, D_HEAD), chunk, F32) * lg)
    return dmask, xi, zeta, g_chunk


def _ret_prompt_kernel(lg_ref, q_ref, k_ref, v_ref, gate_ref, gn_ref, y_ref, s_out_ref,
                       s_scr, dmask_scr, xi_scr, zeta_scr, gch_scr, *, n_ck):
    c = pl.program_id(1)

    @pl.when(c == 0)
    def _():
        s_scr[...] = jnp.zeros_like(s_scr)
        for h in range(H_RET):
            dmask, xi, zeta, g_chunk = _decay_terms(RET_CHUNK, RET_CHUNK, lg_ref[h])
            dmask_scr[h] = dmask
            xi_scr[h] = jnp.broadcast_to(xi, (RET_CHUNK, D_HEAD))
            zeta_scr[h] = jnp.broadcast_to(zeta, (RET_CHUNK, D_HEAD))
            gch_scr[h] = jnp.broadcast_to(g_chunk, (8, D_HEAD))

    for h in range(H_RET):
        sl = slice(h * D_HEAD, (h + 1) * D_HEAD)
        s = s_scr[h]
        for a in range(n_ck):
            rows = slice(a * RET_CHUNK, (a + 1) * RET_CHUNK)
            q = q_ref[rows, sl]
            k = k_ref[rows, sl]
            v = v_ref[rows, sl]
            inner = _nt(q, k) * dmask_scr[h]
            o = jnp.dot(inner.astype(BF16), v, preferred_element_type=F32)
            o = o + jnp.dot(q, s.astype(BF16), preferred_element_type=F32) * xi_scr[h]
            kz = (k.astype(F32) * zeta_scr[h]).astype(BF16)
            s = s * gch_scr[h, 0:1, :] + _tn(kz, v)
            y_ref[rows, sl] = _group_norm_gate(o, gn_ref[h:h + 1, :], gate_ref[rows, sl]).astype(y_ref.dtype)
        s_scr[h] = s

    @pl.when(c == pl.num_programs(1) - 1)
    def _():
        s_out_ref[...] = s_scr[...]


def _ret_prompt(lg, qr, kr, vr, gr, gn, batch, seq, n_ck=4):
    n_chunk = seq // (RET_CHUNK * n_ck)
    d_grp = H_RET * D_HEAD
    blk = pl.BlockSpec((RET_CHUNK * n_ck, d_grp), lambda b, c: (b * n_chunk + c, 0))
    tile = pltpu.VMEM((H_RET, RET_CHUNK, D_HEAD), F32)
    return pl.pallas_call(
        functools.partial(_ret_prompt_kernel, n_ck=n_ck),
        grid=(batch, n_chunk),
        in_specs=[
            pl.BlockSpec(memory_space=pltpu.SMEM),
            blk, blk, blk, blk,
            pl.BlockSpec((H_RET, D_HEAD), lambda b, c: (0, 0)),
        ],
        out_specs=[
            blk,
            pl.BlockSpec((None, H_RET, D_HEAD, D_HEAD), lambda b, c: (b, 0, 0, 0)),
        ],
        out_shape=[
            jax.ShapeDtypeStruct((batch * seq, d_grp), BF16),
            jax.ShapeDtypeStruct((batch, H_RET, D_HEAD, D_HEAD), F32),
        ],
        scratch_shapes=[pltpu.VMEM((H_RET, D_HEAD, D_HEAD), F32), tile, tile, tile,
                        pltpu.VMEM((H_RET, 8, D_HEAD), F32)],
        compiler_params=_cparams(("parallel", "arbitrary")),
        name="ret_prompt",
    )(lg, qr, kr, vr, gr, gn)


def _ret_sample_kernel(lg_ref, q_ref, k_ref, v_ref, gate_ref, gn_ref, s_ref, y_ref, s_out_ref, *, n_seq, t):
    n_rows = n_seq * t
    seq_of_row = lax.broadcasted_iota(jnp.int32, (n_rows, 1), 0) // t
    for h in range(H_RET):
        sl = slice(h * D_HEAD, (h + 1) * D_HEAD)
        lg = lg_ref[h]
        dmask, xi, zeta, g_chunk = _decay_terms(n_rows, t, lg)
        qf = q_ref[:, sl]
        q = qf.astype(BF16)
        v = v_ref[:, sl].astype(BF16)
        kf = k_ref[:, sl]
        inner = _nt(q, kf.astype(BF16)) * dmask
        o = jnp.dot(inner.astype(BF16), v, preferred_element_type=F32)
        kz = kf * zeta
        cross = []
        for b in range(n_seq):
            s_old = s_ref[b, h]
            q_b = qf[b * t:(b + 1) * t, :].astype(BF16)
            cross.append(jnp.dot(q_b, s_old.astype(BF16), preferred_element_type=F32))
            kz_b = jnp.where(seq_of_row == b, kz, 0.0).astype(BF16)
            s_out_ref[b, h] = s_old * g_chunk + _tn(kz_b, v)
        o = o + jnp.concatenate(cross, axis=0) * xi
        y_ref[:, sl] = _group_norm_gate(o, gn_ref[h:h + 1, :], gate_ref[:, sl]).astype(y_ref.dtype)


def _ret_sample(lg, qr, kr, vr, gr, gn, state, n_batch, t, n_seq):
    d_grp = H_RET * D_HEAD
    blk = pl.BlockSpec((n_seq * t, d_grp), lambda i: (i, 0))
    sblk = pl.BlockSpec((n_seq, H_RET, D_HEAD, D_HEAD), lambda i: (i, 0, 0, 0))
    return pl.pallas_call(
        functools.partial(_ret_sample_kernel, n_seq=n_seq, t=t),
        grid=(n_batch // n_seq,),
        in_specs=[
            pl.BlockSpec(memory_space=pltpu.SMEM),
            blk, blk, blk, blk,
            pl.BlockSpec((H_RET, D_HEAD), lambda i: (0, 0)),
            sblk,
        ],
        out_specs=[blk, sblk],
        out_shape=[
            jax.ShapeDtypeStruct((n_batch * t, d_grp), BF16),
            jax.ShapeDtypeStruct((n_batch, H_RET, D_HEAD, D_HEAD), F32),
        ],
        compiler_params=_cparams(("parallel",)),
        name="ret_sample",
    )(lg, qr, kr, vr, gr, gn, state)


def _lambda(lq1_ref, lk1_ref, lq2_ref, lk2_ref, lam_init):
    a = jnp.sum(lq1_ref[...] * lk1_ref[...], axis=-1, keepdims=True)
    b = jnp.sum(lq2_ref[...] * lk2_ref[...], axis=-1, keepdims=True)
    return jnp.exp(a) - jnp.exp(b) + lam_init


def _diff_finish(o, gain, gate, lam_init):
    y = o * lax.rsqrt(jnp.mean(o * o, axis=-1, keepdims=True) + NORM_EPS)
    return y * gain * (1.0 - lam_init) * _silu(gate)


def _map_split(q):
    lane = lax.broadcasted_iota(jnp.int32, q.shape, 1)
    zero = jnp.zeros_like(q)
    return jnp.concatenate([jnp.where(lane < DK_DIFF, q, zero), jnp.where(lane >= DK_DIFF, q, zero)], axis=0)


def _diff_prompt_tile(h, qi, lam, slope_ref, q_ref, k_ref, vt_ref, gate_ref, gain_ref, y_ref,
                      m_ref, l_ref, acc_ref, qbd_ref, sa_ref, sb_ref, *, tq, tk, cw, lam_init):
    slope = slope_ref[h]
    lane_q = lax.broadcasted_iota(jnp.int32, (2 * tq, D_HEAD), 1)
    qbd_ref[:, :D_HEAD] = _map_split(q_ref[...])
    qbd_ref[:, D_HEAD:] = jnp.where(lane_q < 2, 1.0, 0.0).astype(BF16)
    row_k = lax.broadcasted_iota(jnp.int32, (tk, D_HEAD), 0)
    lane_k = lax.broadcasted_iota(jnp.int32, (tk, D_HEAD), 1)
    j_part = jnp.where(lane_k == 0, (row_k // 16) * 16, jnp.where(lane_k == 1, row_k % 16, 0))
    k_aux = (j_part.astype(F32) * slope).astype(BF16)
    kml = lax.broadcasted_iota(jnp.int32, (tk, cw), 0) - lax.broadcasted_iota(jnp.int32, (tk, cw), 1)

    m_ref[...] = jnp.full_like(m_ref, NEG_INF)
    l_ref[...] = jnp.zeros_like(l_ref)
    acc_ref[...] = jnp.zeros_like(acc_ref)

    def scores(ki, s_ref):
        start = pl.multiple_of(ki * tk, tk)
        k_aug = jnp.concatenate([k_ref[pl.ds(start, tk), :], k_aux], axis=1)
        s_ref[...] = _nt(k_aug, qbd_ref[...])

    def absorb(ki, s_ref, diag_off):
        start = pl.multiple_of(ki * tk, tk)
        vt = vt_ref[:, pl.ds(start, tk)]
        for c in range(2 * tq // cw):
            cols = slice(c * cw, (c + 1) * cw)
            q0 = (c * cw) % tq
            if diag_off is not None and diag_off >= q0 + cw:
                continue
            s = s_ref[:, cols]
            if diag_off is not None:
                s = jnp.where(kml <= q0 - diag_off, s, NEG_INF)
            off = slope * (qi * tq - ki * tk).astype(F32)
            m = m_ref[:, cols]
            m_new = jnp.maximum(m, jnp.max(s, axis=0, keepdims=True) - off)
            corr = jnp.exp(m - m_new)
            p = jnp.exp(s - (m_new + off))
            l_ref[:, cols] = l_ref[:, cols] * corr + jnp.sum(p, axis=0, keepdims=True)
            acc_ref[:, cols] = acc_ref[:, cols] * corr + jnp.dot(vt, p.astype(BF16), preferred_element_type=F32)
            m_ref[:, cols] = m_new

    assert tq == 2 * tk
    n_full = 2 * qi
    scores(0, sa_ref)

    def body(pi, carry):
        scores(2 * pi + 1, sb_ref)
        absorb(2 * pi, sa_ref, None)
        scores(2 * pi + 2, sa_ref)
        absorb(2 * pi + 1, sb_ref, None)
        return carry

    lax.fori_loop(0, qi, body, 0)
    scores(n_full + 1, sb_ref)
    absorb(n_full, sa_ref, 0)
    absorb(n_full + 1, sb_ref, tk)

    o_t = acc_ref[...] / l_ref[...]
    o = (o_t[:, :tq] - lam * o_t[:, tq:]).T
    y_ref[...] = _diff_finish(o, gain_ref[pl.ds(h, 1), :], gate_ref[...], lam_init).astype(y_ref.dtype)


def _diff_sample_sequence(seq, lam, pt_ref, q_ref, kl_ref, vl_ref, gate_ref, gain_ref, ck_hbm, cv_hbm, y_ref,
                          qpair_ref, kpad_ref, vpad_ref, bias_ref, m_ref, l_ref, acc_ref, kbuf, vbuf, sem,
                          *, t, n_seq, n_pp, n_grp, n_slots, n_steps, past, lam_init):
    n_total = n_seq * n_steps
    rows_h = 2 * t
    n_rows = H_DIFF * rows_h
    ppg = n_pp // n_grp

    def batch_copies(src_step, slot, known_pages):
        for i in range(n_pp):
            page = pt_ref[src_step * n_pp + i] if known_pages else 0
            yield pltpu.make_async_copy(ck_hbm.at[page], kbuf.at[slot, i], sem.at[slot])
            yield pltpu.make_async_copy(cv_hbm.at[page], vbuf.at[slot, i], sem.at[slot])

    def start_batch(src_step):
        for cp in batch_copies(src_step, src_step % n_slots, True):
            cp.start()

    @pl.when(seq == 0)
    def _():
        for d in range(n_slots - 1):
            start_batch(d)

    def alibi(width, causal):
        rowi = lax.broadcasted_iota(jnp.int32, (n_rows, width), 0)
        tok_k = lax.broadcasted_iota(jnp.int32, (n_rows, width), 1)
        slope = jnp.exp2(-8.0 * ((rowi // rows_h).astype(F32) + 1.0) / H_DIFF)
        tok_q = rowi % t
        bias = -slope * (tok_q - tok_k).astype(F32)
        if causal:
            bias = jnp.where(tok_k <= tok_q, bias, NEG_INF)
        return bias

    def update(g, k_heads, v_heads, bias, shift):
        pairs = range(0, H_DIFF, 2)
        sa = jnp.concatenate(
            [_nt(qpair_ref[h // 2], jnp.concatenate([k_heads[h], k_heads[h + 1]], axis=1)) for h in pairs],
            axis=0) + bias
        m = m_ref[g]
        m_new = jnp.maximum(m, jnp.max(sa, axis=-1, keepdims=True) - shift)
        corr = jnp.exp(m - m_new)
        p = jnp.exp(sa - (m_new + shift))
        l_ref[g] = l_ref[g] * corr + jnp.sum(p, axis=-1, keepdims=True)
        pv = []
        for h in pairs:
            both = jnp.dot(p[h * rows_h:(h + 2) * rows_h, :].astype(BF16),
                           jnp.concatenate([v_heads[h], v_heads[h + 1]], axis=1), preferred_element_type=F32)
            pv += [both[:rows_h, :D_HEAD], both[rows_h:, D_HEAD:]]
        acc_ref[g] = acc_ref[g] * corr + jnp.concatenate(pv, axis=0)
        m_ref[g] = m_new

    rowi = lax.broadcasted_iota(jnp.int32, (n_rows, 1), 0)
    slope = jnp.exp2(-8.0 * ((rowi // rows_h).astype(F32) + 1.0) / H_DIFF)

    @pl.when(seq == 0)
    def _():
        bias_ref[:, :PAGE] = alibi(PAGE, causal=True) + slope * float(past)
        bias_ref[:, PAGE:] = alibi(ppg * PAGE, causal=False)
        kpad_ref[...] = jnp.zeros_like(kpad_ref)
        vpad_ref[...] = jnp.zeros_like(vpad_ref)

    for h in range(0, H_DIFF, 2):
        q0 = _map_split(q_ref[:, h * D_HEAD:(h + 1) * D_HEAD])
        q1 = _map_split(q_ref[:, (h + 1) * D_HEAD:(h + 2) * D_HEAD])
        zero = jnp.zeros_like(q0)
        qpair_ref[h // 2] = jnp.concatenate(
            [jnp.concatenate([q0, zero], axis=1), jnp.concatenate([zero, q1], axis=1)], axis=0).astype(BF16)
    kpad_ref[0:t, :] = kl_ref[...]
    vpad_ref[0:t, :] = vl_ref[...]
    m_ref[...] = jnp.full_like(m_ref, NEG_INF)
    l_ref[...] = jnp.zeros_like(l_ref)
    acc_ref[...] = jnp.zeros_like(acc_ref)

    def head_rows(ref, h):
        return ref[pl.ds(h, PAGE, stride=H_DIFF), :].astype(BF16)

    def new_token_rows(ref, h):
        return ref[:, h * D_HEAD:(h + 1) * D_HEAD].astype(BF16)

    for j in range(n_steps):
        step = seq * n_steps + j

        @pl.when(step + (n_slots - 1) < n_total)
        def _():
            start_batch(step + (n_slots - 1))

        slot = step % n_slots
        for cp in batch_copies(step, slot, False):
            cp.wait()
        for g in range(n_grp):
            pages = range(g * ppg, (g + 1) * ppg)
            first_key = (j * n_pp + g * ppg) * PAGE
            shift = slope * float(past - first_key)
            first = j == 0 and g == 0
            k_heads = [jnp.concatenate(([new_token_rows(kpad_ref, h)] if first else [])
                                       + [head_rows(kbuf.at[slot, i], h) for i in pages], axis=0)
                       for h in range(H_DIFF)]
            v_heads = [jnp.concatenate(([new_token_rows(vpad_ref, h)] if first else [])
                                       + [head_rows(vbuf.at[slot, i], h) for i in pages], axis=0)
                       for h in range(H_DIFF)]
            update(g, k_heads, v_heads, bias_ref[...] if first else bias_ref[:, PAGE:], shift)

    m_all = m_ref[0]
    for g in range(1, n_grp):
        m_all = jnp.maximum(m_all, m_ref[g])
    l_all = jnp.zeros_like(m_all)
    acc = jnp.zeros((n_rows, D_HEAD), F32)
    for g in range(n_grp):
        w = jnp.exp(m_ref[g] - m_all)
        l_all = l_all + l_ref[g] * w
        acc = acc + acc_ref[g] * w
    o_all = acc / l_all
    for h in range(H_DIFF):
        sl = slice(h * D_HEAD, (h + 1) * D_HEAD)
        o = o_all[h * rows_h:h * rows_h + t, :] - lam * o_all[h * rows_h + t:(h + 1) * rows_h, :]
        y_ref[:, sl] = _diff_finish(o, gain_ref[h:h + 1, :], gate_ref[:, sl], lam_init).astype(y_ref.dtype)


def _diff_attention_kernel(pt_ref, slope_ref, qp_ref, kp_ref, vtp_ref, gatep_ref,
                           qs_ref, kls_ref, vls_ref, gates_ref, gain_ref,
                           lq1_ref, lk1_ref, lq2_ref, lk2_ref, ck_hbm, cv_hbm, yp_ref, ys_ref,
                           m_ref, l_ref, acc_ref, qbd_ref, sa_ref, sb_ref,
                           qpair_ref, kpad_ref, vpad_ref, bias_ref, ms_ref, ls_ref, accs_ref, kbuf, vbuf, sem,
                           *, nq, prompt_kw, sample_kw):
    g = pl.program_id(0)
    lam = _lambda(lq1_ref, lk1_ref, lq2_ref, lk2_ref, prompt_kw["lam_init"])
    _diff_sample_sequence(g, lam, pt_ref, qs_ref, kls_ref, vls_ref, gates_ref, gain_ref,
                          ck_hbm, cv_hbm, ys_ref, qpair_ref, kpad_ref, vpad_ref, bias_ref, ms_ref, ls_ref,
                          accs_ref, kbuf, vbuf, sem, **sample_kw)
    h = (g // nq) % H_DIFF
    _diff_prompt_tile(h, _tile_order(g % nq, nq), lam, slope_ref, qp_ref, kp_ref, vtp_ref, gatep_ref, gain_ref,
                      yp_ref, m_ref, l_ref, acc_ref, qbd_ref, sa_ref, sb_ref, **prompt_kw)


def _tile_order(i, nq):
    return jnp.where(i % 2 == 0, nq - 1 - i // 2, i // 2)


def _diff_attention(slopes, qd, kd16, vdt16, gd, qd_s, kd_s, vd_s, gd_s, gain, lams, page_table, cache_k, cache_v,
                    batch, seq, n_dec, t, lam_init, tq=512, tk=256, cw=256, n_pp=16, n_grp=1, n_slots=4):
    assert tq % cw == 0
    nq = seq // tq
    n_pages = page_table.shape[1]
    past = n_pages * PAGE
    d_grp = H_DIFF * D_HEAD
    n_rows = H_DIFF * 2 * t
    n_steps = n_pages // n_pp
    assert n_dec == batch * H_DIFF * nq and n_pages % n_pp == 0 and n_dec * n_steps >= n_slots - 1
    pt_flat = page_table.reshape(-1)
    page_cols = PAGE * H_DIFF
    n_pool = cache_k.shape[1]
    cache_k = cache_k.reshape(n_pool, page_cols, D_HEAD)
    cache_v = cache_v.reshape(n_pool, page_cols, D_HEAD)

    def tile(g):
        return g // (H_DIFF * nq), (g // nq) % H_DIFF, _tile_order(g % nq, nq)

    def q_map(g, pt):
        b, h, qi = tile(g)
        return b * nq + qi, h

    qblk = pl.BlockSpec((tq, D_HEAD), q_map)
    tokblk = pl.BlockSpec((t, d_grp), lambda g, pt: (g, 0))
    vec = pl.BlockSpec((1, DK_DIFF), lambda g, pt: (0, 0))
    hbm = pl.BlockSpec(memory_space=pl.ANY)
    grid_spec = pltpu.PrefetchScalarGridSpec(
        num_scalar_prefetch=1,
        grid=(n_dec,),
        in_specs=[
            pl.BlockSpec(memory_space=pltpu.SMEM),
            qblk,
            pl.BlockSpec((seq, D_HEAD), lambda g, pt: (tile(g)[0], tile(g)[1])),
            pl.BlockSpec((D_HEAD, seq), lambda g, pt: (tile(g)[1], tile(g)[0])),
            qblk,
            tokblk, tokblk, tokblk, tokblk,
            pl.BlockSpec((H_DIFF, D_HEAD), lambda g, pt: (0, 0)),
            vec, vec, vec, vec, hbm, hbm,
        ],
        out_specs=[qblk, tokblk],
        scratch_shapes=[
            pltpu.VMEM((1, 2 * tq), F32),
            pltpu.VMEM((1, 2 * tq), F32),
            pltpu.VMEM((D_HEAD, 2 * tq), F32),
            pltpu.VMEM((2 * tq, 2 * D_HEAD), BF16),
            pltpu.VMEM((tk, 2 * tq), F32),
            pltpu.VMEM((tk, 2 * tq), F32),
            pltpu.VMEM((H_DIFF // 2, n_rows // 2, 2 * D_HEAD), BF16),
            pltpu.VMEM((PAGE, d_grp), F32),
            pltpu.VMEM((PAGE, d_grp), F32),
            pltpu.VMEM((n_rows, (1 + n_pp // n_grp) * PAGE), F32),
            pltpu.VMEM((n_grp, n_rows, 1), F32),
            pltpu.VMEM((n_grp, n_rows, 1), F32),
            pltpu.VMEM((n_grp, n_rows, D_HEAD), F32),
            pltpu.VMEM((n_slots, n_pp, page_cols, D_HEAD), F32),
            pltpu.VMEM((n_slots, n_pp, page_cols, D_HEAD), F32),
            pltpu.SemaphoreType.DMA((n_slots,)),
        ],
    )
    return pl.pallas_call(
        functools.partial(
            _diff_attention_kernel, nq=nq,
            prompt_kw=dict(tq=tq, tk=tk, cw=cw, lam_init=lam_init),
            sample_kw=dict(t=t, n_seq=n_dec, n_pp=n_pp, n_grp=n_grp, n_slots=n_slots, n_steps=n_steps, past=past,
                           lam_init=lam_init)),
        grid_spec=grid_spec,
        out_shape=[jax.ShapeDtypeStruct((batch * seq, d_grp), BF16),
                   jax.ShapeDtypeStruct((n_dec * t, d_grp), F32)],
        compiler_params=_cparams(("arbitrary",)),
        name="diff_attention",
    )(pt_flat, slopes, qd, kd16, vdt16, gd, qd_s, kd_s, vd_s, gd_s, gain, *lams, cache_k, cache_v)


def _finish_kernel(x_ref, yr_ref, yd_ref, w_ref, g_ref, o_ref):
    d_grp = yr_ref.shape[1]
    y = jnp.dot(yr_ref[...].astype(BF16), w_ref[:d_grp, :], preferred_element_type=F32)
    y = y + jnp.dot(yd_ref[...].astype(BF16), w_ref[d_grp:, :], preferred_element_type=F32)
    yn = y * lax.rsqrt(jnp.mean(y * y, axis=-1, keepdims=True) + NORM_EPS) * g_ref[...]
    o_ref[...] = x_ref[...] + yn


def _finish(x, y_ret, y_diff, w_bf16, norm_post, tm):
    n, d_model = x.shape
    d_grp = y_ret.shape[1]
    tok = lambda i: (i, 0)
    return pl.pallas_call(
        _finish_kernel,
        grid=(n // tm,),
        in_specs=[
            pl.BlockSpec((tm, d_model), tok),
            pl.BlockSpec((tm, d_grp), tok),
            pl.BlockSpec((tm, d_grp), tok),
            pl.BlockSpec((2 * d_grp, d_model), lambda i: (0, 0)),
            pl.BlockSpec((1, d_model), lambda i: (0, 0)),
        ],
        out_specs=pl.BlockSpec((tm, d_model), tok),
        out_shape=jax.ShapeDtypeStruct((n, d_model), F32),
        compiler_params=_cparams(("parallel",)),
        name="finish",
    )(x, y_ret, y_diff, w_bf16, norm_post.reshape(1, d_model))


def kernel(x_prompt, x_sample, cache_k, cache_v, page_table, state_ret, norm_pre, norm_post, w_in,
           ret_gn, diff_lq1, diff_lk1, diff_lq2, diff_lk2, diff_norm, w_out):
    batch, seq, d_model = x_prompt.shape
    n_dec, t_dec, _ = x_sample.shape
    depth = w_in.shape[0]
    assert depth == 1
    n_pages = page_table.shape[1]
    past = n_pages * PAGE
    layer = 0
    lam_init = 0.8 - 0.6 * math.exp(-0.3 * layer)

    lg = jnp.log(1.0 - jnp.exp2(-5.0 - jnp.arange(H_RET, dtype=F32)))
    assert 8 % H_DIFF == 0
    slope_exponents = -(8 // H_DIFF) * (jnp.arange(H_DIFF, dtype=jnp.int32) + 1)
    slopes = lax.bitcast_convert_type((slope_exponents + 127) << 23, F32)
    w_in16 = w_in[layer].astype(BF16)
    w_out16 = w_out[layer].astype(BF16)
    lams = [p[layer].reshape(1, DK_DIFF) for p in (diff_lq1, diff_lk1, diff_lq2, diff_lk2)]
    tm = 512

    xp = x_prompt.reshape(batch * seq, d_model)
    cos_p, sin_p = _rope_tables(jnp.arange(seq))
    qr, kr, vr, gr, qd, kd, vd, gd, kd16, vdt16 = _project(
        xp, norm_pre[layer], w_in16, cos_p, sin_p, tm, BF16, batch_seq=(batch, seq))
    xs = x_sample.reshape(n_dec * t_dec, d_model)
    cos_s, sin_s = _rope_tables(past + jnp.arange(tm) % t_dec)
    qr_s, kr_s, vr_s, gr_s, qd_s, kd_s, vd_s, gd_s, _, _ = _project(xs, norm_pre[layer], w_in16, cos_s, sin_s, tm, F32)

    y_ret, s_fin = _ret_prompt(lg, qr, kr, vr, gr, ret_gn[layer], batch, seq)
    y_ret_s, s_new = _ret_sample(lg, qr_s, kr_s, vr_s, gr_s, ret_gn[layer], state_ret[layer], n_dec, t_dec, n_seq=16)
    y_diff, y_diff_s = _diff_attention(slopes, qd, kd16, vdt16, gd, qd_s, kd_s, vd_s, gd_s, diff_norm[layer], lams,
                                       page_table, cache_k, cache_v, batch, seq, n_dec, t_dec, lam_init)
    out_p = _finish(xp, y_ret, y_diff, w_out16, norm_post[layer], tm)
    out_s = _finish(xs, y_ret_s, y_diff_s, w_out16, norm_post[layer], tm)

    return (
        out_p.reshape(batch, seq, d_model),
        out_s.reshape(n_dec, t_dec, d_model),
        kd,
        vd,
        s_fin.reshape(1, batch, H_RET, D_HEAD, D_HEAD),
        kd_s.reshape(1, n_dec, t_dec, H_DIFF, D_HEAD),
        vd_s.reshape(1, n_dec, t_dec, H_DIFF, D_HEAD),
        s_new.reshape(1, n_dec, H_RET, D_HEAD, D_HEAD),
    )
```

```python
import functools
import math

import jax
import jax.numpy as jnp
from jax import lax
from jax.experimental import pallas as pl
from jax.experimental.pallas import tpu as pltpu

F32 = jnp.float32
BF16 = jnp.bfloat16

H_RET = 4
H_DIFF = 4
D_HEAD = 128
DK_DIFF = 64
RET_CHUNK = 128
PAGE = 128
ROPE_BASE = 10000.0
NORM_EPS = 1e-6
GN_EPS = 1e-5
NEG_INF = float("-inf")

VMEM_LIMIT = 56 * 1024 * 1024


def _cparams(sem):
    return pltpu.CompilerParams(dimension_semantics=sem, vmem_limit_bytes=VMEM_LIMIT)


def _silu(x):
    return x * (1.0 / (1.0 + jnp.exp(-x)))


def _nt(a, b):
    return lax.dot_general(a, b, (((1,), (1,)), ((), ())), preferred_element_type=F32)


def _tn(a, b):
    return lax.dot_general(a, b, (((0,), (0,)), ((), ())), preferred_element_type=F32)


def _proj_kernel(x_ref, g_ref, w_ref, cos_ref, sin_ref,
                 qr_ref, kr_ref, vr_ref, gr_ref, qd_ref, kd_ref, vd_ref, gd_ref, kd16_ref, vdt16_ref):
    d_grp = H_RET * D_HEAD
    x = x_ref[...]
    ms = jnp.mean(x * x, axis=-1, keepdims=True)
    h = (x * lax.rsqrt(ms + NORM_EPS) * g_ref[...]).astype(BF16)
    cos2 = cos_ref[...]
    sin2 = sin_ref[...]

    def section(s):
        return jnp.dot(h, w_ref[:, s * d_grp:(s + 1) * d_grp], preferred_element_type=F32)

    def rope(z, scale):
        for hd in range(H_RET):
            zh = z[:, hd * D_HEAD:(hd + 1) * D_HEAD]
            rot = pltpu.roll(zh, D_HEAD // 2, axis=1)
            yield hd, (zh * cos2 + rot * sin2) * scale

    def store_heads(ref, z):
        if len(ref.shape) == 2:
            ref[...] = z
        else:
            for hd in range(H_DIFF):
                ref[:, hd, :] = z[:, hd * D_HEAD:(hd + 1) * D_HEAD]

    z = section(0)
    for hd, r in rope(z, 1.0):
        qr_ref[:, hd * D_HEAD:(hd + 1) * D_HEAD] = r.astype(qr_ref.dtype)
    z = section(1)
    for hd, r in rope(z, D_HEAD ** -0.5):
        kr_ref[:, hd * D_HEAD:(hd + 1) * D_HEAD] = r.astype(kr_ref.dtype)
    vr_ref[...] = section(2).astype(vr_ref.dtype)
    gr_ref[...] = section(3)
    qd_ref[...] = (section(4) * (DK_DIFF ** -0.5)).astype(qd_ref.dtype)
    z = section(5)
    store_heads(kd_ref, z)
    kd16_ref[...] = z.astype(BF16)
    z = section(6)
    store_heads(vd_ref, z)
    vdt16_ref[...] = z.T.astype(BF16)
    gd_ref[...] = section(7)


def _project(x, norm_pre, w_bf16, cos2, sin2, tm, ret_dtype, batch_seq=None):
    n, d_model = x.shape
    d_in = w_bf16.shape[1]
    d_grp = d_in // 8
    n_tab = cos2.shape[0] // tm
    tok = lambda i: (i, 0)
    out_blk = pl.BlockSpec((tm, d_grp), tok)
    if batch_seq is None:
        kv_shape = jax.ShapeDtypeStruct((n, d_grp), F32)
        kv_blk = out_blk
    else:
        batch, seq = batch_seq
        per_seq = seq // tm
        kv_shape = jax.ShapeDtypeStruct((1, batch, seq, H_DIFF, D_HEAD), F32)
        kv_blk = pl.BlockSpec((None, None, tm, H_DIFF, D_HEAD), lambda i: (0, i // per_seq, i % per_seq, 0, 0))
    shapes = [
        jax.ShapeDtypeStruct((n, d_grp), ret_dtype),
        jax.ShapeDtypeStruct((n, d_grp), ret_dtype),
        jax.ShapeDtypeStruct((n, d_grp), ret_dtype),
        jax.ShapeDtypeStruct((n, d_grp), F32),
        jax.ShapeDtypeStruct((n, d_grp), ret_dtype),
        kv_shape,
        kv_shape,
        jax.ShapeDtypeStruct((n, d_grp), F32),
        jax.ShapeDtypeStruct((n, d_grp), BF16),
        jax.ShapeDtypeStruct((d_grp, n), BF16),
    ]
    out_specs = [out_blk] * 5 + [kv_blk, kv_blk, out_blk, out_blk, pl.BlockSpec((d_grp, tm), lambda i: (0, i))]
    return pl.pallas_call(
        _proj_kernel,
        grid=(n // tm,),
        in_specs=[
            pl.BlockSpec((tm, d_model), tok),
            pl.BlockSpec((1, d_model), lambda i: (0, 0)),
            pl.BlockSpec((d_model, d_in), lambda i: (0, 0)),
            pl.BlockSpec((tm, D_HEAD), lambda i: (i % n_tab, 0)),
            pl.BlockSpec((tm, D_HEAD), lambda i: (i % n_tab, 0)),
        ],
        out_specs=out_specs,
        out_shape=shapes,
        compiler_params=_cparams(("parallel",)),
        name="proj",
    )(x, norm_pre.reshape(1, d_model), w_bf16, cos2, sin2)


def _rope_tables(pos):
    half = D_HEAD // 2
    inv = ROPE_BASE ** (-jnp.arange(half, dtype=F32) / half)
    ang = pos.astype(F32)[:, None] * inv[None, :]
    cos, sin = jnp.cos(ang), jnp.sin(ang)
    return jnp.concatenate([cos, cos], axis=-1), jnp.concatenate([-sin, sin], axis=-1)


def _group_norm_gate(o, gain, gate):
    mu = jnp.mean(o, axis=-1, keepdims=True)
    d = o - mu
    var = jnp.mean(d * d, axis=-1, keepdims=True)
    return d * lax.rsqrt(var + GN_EPS) * gain * _silu(gate)


def _decay_terms(n_rows, chunk, lg):
    row = lax.broadcasted_iota(jnp.int32, (n_rows, n_rows), 0)
    col = lax.broadcasted_iota(jnp.int32, (n_rows, n_rows), 1)
    dist = (row - col).astype(F32)
    dmask = jnp.where(col <= row, jnp.exp(jnp.maximum(dist, 0.0) * lg), 0.0)
    if n_rows != chunk:
        dmask = jnp.where(row // chunk == col // chunk, dmask, 0.0)
    idx = (lax.broadcasted_iota(jnp.int32, (n_rows, 1), 0) % chunk).astype(F32)
    xi = jnp.exp((idx + 1.0) * lg)
    zeta = jnp.exp((chunk - 1.0 - idx) * lg)
    g_chunk = jnp.exp(jnp.full((1, D_HEAD), chunk, F32) * lg)
    return dmask, xi, zeta, g_chunk


def _ret_prompt_kernel(lg_ref, q_ref, k_ref, v_ref, gate_ref, gn_ref, y_ref, s_out_ref,
                       s_scr, dmask_scr, xi_scr, zeta_scr, gch_scr, *, n_ck):
    c = pl.program_id(1)

    @pl.when(c == 0)
    def _():
        s_scr[...] = jnp.zeros_like(s_scr)
        for h in range(H_RET):
            dmask, xi, zeta, g_chunk = _decay_terms(RET_CHUNK, RET_CHUNK, lg_ref[h])
            dmask_scr[h] = dmask
            xi_scr[h] = jnp.broadcast_to(xi, (RET_CHUNK, D_HEAD))
            zeta_scr[h] = jnp.broadcast_to(zeta, (RET_CHUNK, D_HEAD))
            gch_scr[h] = jnp.broadcast_to(g_chunk, (8, D_HEAD))

    for h in range(H_RET):
        sl = slice(h * D_HEAD, (h + 1) * D_HEAD)
        s = s_scr[h]
        for a in range(n_ck):
            rows = slice(a * RET_CHUNK, (a + 1) * RET_CHUNK)
            q = q_ref[rows, sl]
            k = k_ref[rows, sl]
            v = v_ref[rows, sl]
            inner = _nt(q, k) * dmask_scr[h]
            o = jnp.dot(inner.astype(BF16), v, preferred_element_type=F32)
            o = o + jnp.dot(q, s.astype(BF16), preferred_element_type=F32) * xi_scr[h]
            kz = (k.astype(F32) * zeta_scr[h]).astype(BF16)
            s = s * gch_scr[h, 0:1, :] + _tn(kz, v)
            y_ref[rows, sl] = _group_norm_gate(o, gn_ref[h:h + 1, :], gate_ref[rows, sl]).astype(y_ref.dtype)
        s_scr[h] = s

    @pl.when(c == pl.num_programs(1) - 1)
    def _():
        s_out_ref[...] = s_scr[...]


def _ret_prompt(lg, qr, kr, vr, gr, gn, batch, seq, n_ck=4):
    n_chunk = seq // (RET_CHUNK * n_ck)
    d_grp = H_RET * D_HEAD
    blk = pl.BlockSpec((RET_CHUNK * n_ck, d_grp), lambda b, c: (b * n_chunk + c, 0))
    tile = pltpu.VMEM((H_RET, RET_CHUNK, D_HEAD), F32)
    return pl.pallas_call(
        functools.partial(_ret_prompt_kernel, n_ck=n_ck),
        grid=(batch, n_chunk),
        in_specs=[
            pl.BlockSpec(memory_space=pltpu.SMEM),
            blk, blk, blk, blk,
            pl.BlockSpec((H_RET, D_HEAD), lambda b, c: (0, 0)),
        ],
        out_specs=[
            blk,
            pl.BlockSpec((None, H_RET, D_HEAD, D_HEAD), lambda b, c: (b, 0, 0, 0)),
        ],
        out_shape=[
            jax.ShapeDtypeStruct((batch * seq, d_grp), BF16),
            jax.ShapeDtypeStruct((batch, H_RET, D_HEAD, D_HEAD), F32),
        ],
        scratch_shapes=[pltpu.VMEM((H_RET, D_HEAD, D_HEAD), F32), tile, tile, tile,
                        pltpu.VMEM((H_RET, 8, D_HEAD), F32)],
        compiler_params=_cparams(("parallel", "arbitrary")),
        name="ret_prompt",
    )(lg, qr, kr, vr, gr, gn)


def _ret_sample_kernel(lg_ref, q_ref, k_ref, v_ref, gate_ref, gn_ref, s_ref, y_ref, s_out_ref, *, n_seq, t):
    n_rows = n_seq * t
    seq_of_row = lax.broadcasted_iota(jnp.int32, (n_rows, 1), 0) // t
    for h in range(H_RET):
        sl = slice(h * D_HEAD, (h + 1) * D_HEAD)
        lg = lg_ref[h]
        dmask, xi, zeta, g_chunk = _decay_terms(n_rows, t, lg)
        qf = q_ref[:, sl]
        q = qf.astype(BF16)
        v = v_ref[:, sl].astype(BF16)
        kf = k_ref[:, sl]
        inner = _nt(q, kf.astype(BF16)) * dmask
        o = jnp.dot(inner.astype(BF16), v, preferred_element_type=F32)
        kz = kf * zeta
        cross = []
        for b in range(n_seq):
            s_old = s_ref[b, h]
            q_b = qf[b * t:(b + 1) * t, :].astype(BF16)
            cross.append(jnp.dot(q_b, s_old.astype(BF16), preferred_element_type=F32))
            kz_b = jnp.where(seq_of_row == b, kz, 0.0).astype(BF16)
            s_out_ref[b, h] = s_old * g_chunk + _tn(kz_b, v)
        o = o + jnp.concatenate(cross, axis=0) * xi
        y_ref[:, sl] = _group_norm_gate(o, gn_ref[h:h + 1, :], gate_ref[:, sl]).astype(y_ref.dtype)


def _ret_sample(lg, qr, kr, vr, gr, gn, state, n_batch, t, n_seq):
    d_grp = H_RET * D_HEAD
    blk = pl.BlockSpec((n_seq * t, d_grp), lambda i: (i, 0))
    sblk = pl.BlockSpec((n_seq, H_RET, D_HEAD, D_HEAD), lambda i: (i, 0, 0, 0))
    return pl.pallas_call(
        functools.partial(_ret_sample_kernel, n_seq=n_seq, t=t),
        grid=(n_batch // n_seq,),
        in_specs=[
            pl.BlockSpec(memory_space=pltpu.SMEM),
            blk, blk, blk, blk,
            pl.BlockSpec((H_RET, D_HEAD), lambda i: (0, 0)),
            sblk,
        ],
        out_specs=[blk, sblk],
        out_shape=[
            jax.ShapeDtypeStruct((n_batch * t, d_grp), BF16),
            jax.ShapeDtypeStruct((n_batch, H_RET, D_HEAD, D_HEAD), F32),
        ],
        compiler_params=_cparams(("parallel",)),
        name="ret_sample",
    )(lg, qr, kr, vr, gr, gn, state)


def _lambda(lq1_ref, lk1_ref, lq2_ref, lk2_ref, lam_init):
    a = jnp.sum(lq1_ref[...] * lk1_ref[...], axis=-1, keepdims=True)
    b = jnp.sum(lq2_ref[...] * lk2_ref[...], axis=-1, keepdims=True)
    return jnp.exp(a) - jnp.exp(b) + lam_init


def _diff_finish(o, gain, gate, lam_init):
    y = o * lax.rsqrt(jnp.mean(o * o, axis=-1, keepdims=True) + NORM_EPS)
    return y * gain * (1.0 - lam_init) * _silu(gate)


def _map_split(q):
    lane = lax.broadcasted_iota(jnp.int32, q.shape, 1)
    zero = jnp.zeros_like(q)
    return jnp.concatenate([jnp.where(lane < DK_DIFF, q, zero), jnp.where(lane >= DK_DIFF, q, zero)], axis=0)


def _diff_prompt_tile(h, qi, lam, slope_ref, q_ref, k_ref, vt_ref, gate_ref, gain_ref, y_ref,
                      m_ref, l_ref, acc_ref, qbd_ref, kaux_ref, sa_ref, sb_ref, *, tq, tk, cw, lam_init):
    slope = slope_ref[h]
    lane_q = lax.broadcasted_iota(jnp.int32, (2 * tq, D_HEAD), 1)
    qbd_ref[:, :D_HEAD] = _map_split(q_ref[...])
    qbd_ref[:, D_HEAD:] = jnp.where(lane_q < 2, 1.0, 0.0).astype(BF16)
    row_k = lax.broadcasted_iota(jnp.int32, (tk, D_HEAD), 0)
    lane_k = lax.broadcasted_iota(jnp.int32, (tk, D_HEAD), 1)
    j_part = jnp.where(lane_k == 0, (row_k // 16) * 16, jnp.where(lane_k == 1, row_k % 16, 0))
    kaux_ref[...] = (j_part.astype(F32) * slope).astype(BF16)

    m_ref[...] = jnp.full_like(m_ref, NEG_INF)
    l_ref[...] = jnp.zeros_like(l_ref)
    acc_ref[...] = jnp.zeros_like(acc_ref)

    def scores(ki, s_ref):
        start = pl.multiple_of(ki * tk, tk)
        k_aug = jnp.concatenate([k_ref[pl.ds(start, tk), :], kaux_ref[...]], axis=1)
        s_ref[...] = _nt(k_aug, qbd_ref[...])

    def absorb(ki, s_ref, diag_off):
        start = pl.multiple_of(ki * tk, tk)
        vt = vt_ref[:, pl.ds(start, tk)]
        if diag_off is not None:
            kml = lax.broadcasted_iota(jnp.int32, (tk, cw), 0) - lax.broadcasted_iota(jnp.int32, (tk, cw), 1)
        for c in range(2 * tq // cw):
            cols = slice(c * cw, (c + 1) * cw)
            q0 = (c * cw) % tq
            if diag_off is not None and diag_off >= q0 + cw:
                continue
            s = s_ref[:, cols]
            if diag_off is not None:
                s = jnp.where(kml <= q0 - diag_off, s, NEG_INF)
            off = slope * (qi * tq - ki * tk).astype(F32)
            m = m_ref[:, cols]
            m_new = jnp.maximum(m, jnp.max(s, axis=0, keepdims=True) - off)
            corr = jnp.exp(m - m_new)
            p = jnp.exp(s - (m_new + off))
            l_ref[:, cols] = l_ref[:, cols] * corr + jnp.sum(p, axis=0, keepdims=True)
            acc_ref[:, cols] = acc_ref[:, cols] * corr + jnp.dot(vt, p.astype(BF16), preferred_element_type=F32)
            m_ref[:, cols] = m_new

    assert tq == 2 * tk
    n_full = 2 * qi
    scores(0, sa_ref)

    def body(pi, carry):
        scores(2 * pi + 1, sb_ref)
        absorb(2 * pi, sa_ref, None)
        scores(2 * pi + 2, sa_ref)
        absorb(2 * pi + 1, sb_ref, None)
        return carry

    def run_pairs(lo, hi):
        lax.fori_loop(lo, hi, body, 0)

    def finish():
        scores(n_full + 1, sb_ref)
        absorb(n_full, sa_ref, 0)
        absorb(n_full + 1, sb_ref, tk)
        o_t = acc_ref[...] / l_ref[...]
        o = (o_t[:, :tq] - lam * o_t[:, tq:]).T
        y_ref[...] = _diff_finish(o, gain_ref[pl.ds(h, 1), :], gate_ref[...], lam_init).astype(y_ref.dtype)

    return run_pairs, finish


def _diff_sample_sequence(seq, lam, pt_ref, q_ref, kl_ref, vl_ref, gate_ref, gain_ref, ck_hbm, cv_hbm, y_ref,
                          qpair_ref, kpad_ref, vpad_ref, bias_ref, m_ref, l_ref, acc_ref, kbuf, vbuf, sem,
                          *, t, n_seq, n_pp, n_grp, n_slots, n_steps, past, lam_init):
    n_total = n_seq * n_steps
    rows_h = 2 * t
    n_rows = H_DIFF * rows_h
    ppg = n_pp // n_grp

    def batch_copies(src_step, slot, known_pages):
        for i in range(n_pp):
            page = pt_ref[src_step * n_pp + i] if known_pages else 0
            yield pltpu.make_async_copy(ck_hbm.at[page], kbuf.at[slot, i], sem.at[slot])
            yield pltpu.make_async_copy(cv_hbm.at[page], vbuf.at[slot, i], sem.at[slot])

    def start_batch(src_step):
        for cp in batch_copies(src_step, src_step % n_slots, True):
            cp.start()

    @pl.when(seq == 0)
    def _():
        for d in range(n_slots - 1):
            start_batch(d)

    def alibi(width, causal):
        rowi = lax.broadcasted_iota(jnp.int32, (n_rows, width), 0)
        tok_k = lax.broadcasted_iota(jnp.int32, (n_rows, width), 1)
        slope = jnp.exp2(-8.0 * ((rowi // rows_h).astype(F32) + 1.0) / H_DIFF)
        tok_q = rowi % t
        bias = -slope * (tok_q - tok_k).astype(F32)
        if causal:
            bias = jnp.where(tok_k <= tok_q, bias, NEG_INF)
        return bias

    def update(g, k_heads, v_heads, bias, shift):
        pairs = range(0, H_DIFF, 2)
        sa = jnp.concatenate(
            [_nt(qpair_ref[h // 2], jnp.concatenate([k_heads[h], k_heads[h + 1]], axis=1)) for h in pairs],
            axis=0) + bias
        m = m_ref[g]
        m_new = jnp.maximum(m, jnp.max(sa, axis=-1, keepdims=True) - shift)
        corr = jnp.exp(m - m_new)
        p = jnp.exp(sa - (m_new + shift))
        l_ref[g] = l_ref[g] * corr + jnp.sum(p, axis=-1, keepdims=True)
        pv = []
        for h in pairs:
            both = jnp.dot(p[h * rows_h:(h + 2) * rows_h, :].astype(BF16),
                           jnp.concatenate([v_heads[h], v_heads[h + 1]], axis=1), preferred_element_type=F32)
            pv += [both[:rows_h, :D_HEAD], both[rows_h:, D_HEAD:]]
        acc_ref[g] = acc_ref[g] * corr + jnp.concatenate(pv, axis=0)
        m_ref[g] = m_new

    rowi = lax.broadcasted_iota(jnp.int32, (n_rows, 1), 0)
    slope = jnp.exp2(-8.0 * ((rowi // rows_h).astype(F32) + 1.0) / H_DIFF)

    @pl.when(seq == 0)
    def _():
        bias_ref[:, :PAGE] = alibi(PAGE, causal=True) + slope * float(past)
        bias_ref[:, PAGE:] = alibi(ppg * PAGE, causal=False)
        kpad_ref[...] = jnp.zeros_like(kpad_ref)
        vpad_ref[...] = jnp.zeros_like(vpad_ref)

    for h in range(0, H_DIFF, 2):
        q0 = _map_split(q_ref[:, h * D_HEAD:(h + 1) * D_HEAD])
        q1 = _map_split(q_ref[:, (h + 1) * D_HEAD:(h + 2) * D_HEAD])
        zero = jnp.zeros_like(q0)
        qpair_ref[h // 2] = jnp.concatenate(
            [jnp.concatenate([q0, zero], axis=1), jnp.concatenate([zero, q1], axis=1)], axis=0).astype(BF16)
    kpad_ref[0:t, :] = kl_ref[...]
    vpad_ref[0:t, :] = vl_ref[...]
    m_ref[...] = jnp.full_like(m_ref, NEG_INF)
    l_ref[...] = jnp.zeros_like(l_ref)
    acc_ref[...] = jnp.zeros_like(acc_ref)

    def head_rows(ref, h):
        return ref[pl.ds(h, PAGE, stride=H_DIFF), :].astype(BF16)

    def new_token_rows(ref, h):
        return ref[:, h * D_HEAD:(h + 1) * D_HEAD].astype(BF16)

    def run_step(j):
        step = seq * n_steps + j

        @pl.when(step + (n_slots - 1) < n_total)
        def _():
            start_batch(step + (n_slots - 1))

        slot = step % n_slots
        for cp in batch_copies(step, slot, False):
            cp.wait()
        for g in range(n_grp):
            pages = range(g * ppg, (g + 1) * ppg)
            first_key = (j * n_pp + g * ppg) * PAGE
            shift = slope * float(past - first_key)
            first = j == 0 and g == 0
            k_heads = [jnp.concatenate(([new_token_rows(kpad_ref, h)] if first else [])
                                       + [head_rows(kbuf.at[slot, i], h) for i in pages], axis=0)
                       for h in range(H_DIFF)]
            v_heads = [jnp.concatenate(([new_token_rows(vpad_ref, h)] if first else [])
                                       + [head_rows(vbuf.at[slot, i], h) for i in pages], axis=0)
                       for h in range(H_DIFF)]
            update(g, k_heads, v_heads, bias_ref[...] if first else bias_ref[:, PAGE:], shift)

    def finish():
        m_all = m_ref[0]
        for g in range(1, n_grp):
            m_all = jnp.maximum(m_all, m_ref[g])
        l_all = jnp.zeros_like(m_all)
        acc = jnp.zeros((n_rows, D_HEAD), F32)
        for g in range(n_grp):
            w = jnp.exp(m_ref[g] - m_all)
            l_all = l_all + l_ref[g] * w
            acc = acc + acc_ref[g] * w
        o_all = acc / l_all
        for h in range(H_DIFF):
            sl = slice(h * D_HEAD, (h + 1) * D_HEAD)
            o = o_all[h * rows_h:h * rows_h + t, :] - lam * o_all[h * rows_h + t:(h + 1) * rows_h, :]
            y_ref[:, sl] = _diff_finish(o, gain_ref[h:h + 1, :], gate_ref[:, sl], lam_init).astype(y_ref.dtype)

    return run_step, finish


def _diff_attention_kernel(pt_ref, slope_ref, qp_ref, kp_ref, vtp_ref, gatep_ref,
                           qs_ref, kls_ref, vls_ref, gates_ref, gain_ref,
                           lq1_ref, lk1_ref, lq2_ref, lk2_ref, ck_hbm, cv_hbm, yp_ref, ys_ref,
                           m_ref, l_ref, acc_ref, qbd_ref, kaux_ref, sa_ref, sb_ref,
                           qpair_ref, kpad_ref, vpad_ref, bias_ref, ms_ref, ls_ref, accs_ref, kbuf, vbuf, sem,
                           *, nq, prompt_kw, sample_kw):
    g = pl.program_id(0)
    lam = _lambda(lq1_ref, lk1_ref, lq2_ref, lk2_ref, prompt_kw["lam_init"])
    page_step, finish_sequence = _diff_sample_sequence(
        g, lam, pt_ref, qs_ref, kls_ref, vls_ref, gates_ref, gain_ref, ck_hbm, cv_hbm, ys_ref,
        qpair_ref, kpad_ref, vpad_ref, bias_ref, ms_ref, ls_ref, accs_ref, kbuf, vbuf, sem, **sample_kw)
    h = (g // nq) % H_DIFF
    qi = _tile_order(g % nq, nq)
    key_pairs, finish_tile = _diff_prompt_tile(
        h, qi, lam, slope_ref, qp_ref, kp_ref, vtp_ref, gatep_ref, gain_ref, yp_ref,
        m_ref, l_ref, acc_ref, qbd_ref, kaux_ref, sa_ref, sb_ref, **prompt_kw)
    n_steps = sample_kw["n_steps"]
    for j in range(n_steps // 2):
        page_step(j)
    key_pairs(0, qi // 2)
    for j in range(n_steps // 2, n_steps):
        page_step(j)
    key_pairs(qi // 2, qi)
    finish_tile()
    finish_sequence()


def _tile_order(i, nq):
    return jnp.where(i % 2 == 0, nq - 1 - i // 2, i // 2)


def _diff_attention(slopes, qd, kd16, vdt16, gd, qd_s, kd_s, vd_s, gd_s, gain, lams, page_table, cache_k, cache_v,
                    batch, seq, n_dec, t, lam_init, tq=512, tk=256, cw=256, n_pp=16, n_grp=1, n_slots=5):
    assert tq % cw == 0
    nq = seq // tq
    n_pages = page_table.shape[1]
    past = n_pages * PAGE
    d_grp = H_DIFF * D_HEAD
    n_rows = H_DIFF * 2 * t
    n_steps = n_pages // n_pp
    assert n_dec == batch * H_DIFF * nq and n_pages % n_pp == 0 and n_dec * n_steps >= n_slots - 1
    pt_flat = page_table.reshape(-1)
    page_cols = PAGE * H_DIFF
    n_pool = cache_k.shape[1]
    cache_k = cache_k.reshape(n_pool, page_cols, D_HEAD)
    cache_v = cache_v.reshape(n_pool, page_cols, D_HEAD)

    def tile(g):
        return g // (H_DIFF * nq), (g // nq) % H_DIFF, _tile_order(g % nq, nq)

    def q_map(g, pt):
        b, h, qi = tile(g)
        return b * nq + qi, h

    qblk = pl.BlockSpec((tq, D_HEAD), q_map)
    tokblk = pl.BlockSpec((t, d_grp), lambda g, pt: (g, 0))
    vec = pl.BlockSpec((1, DK_DIFF), lambda g, pt: (0, 0))
    hbm = pl.BlockSpec(memory_space=pl.ANY)
    grid_spec = pltpu.PrefetchScalarGridSpec(
        num_scalar_prefetch=1,
        grid=(n_dec,),
        in_specs=[
            pl.BlockSpec(memory_space=pltpu.SMEM),
            qblk,
            pl.BlockSpec((seq, D_HEAD), lambda g, pt: (tile(g)[0], tile(g)[1])),
            pl.BlockSpec((D_HEAD, seq), lambda g, pt: (tile(g)[1], tile(g)[0])),
            qblk,
            tokblk, tokblk, tokblk, tokblk,
            pl.BlockSpec((H_DIFF, D_HEAD), lambda g, pt: (0, 0)),
            vec, vec, vec, vec, hbm, hbm,
        ],
        out_specs=[qblk, tokblk],
        scratch_shapes=[
            pltpu.VMEM((1, 2 * tq), F32),
            pltpu.VMEM((1, 2 * tq), F32),
            pltpu.VMEM((D_HEAD, 2 * tq), F32),
            pltpu.VMEM((2 * tq, 2 * D_HEAD), BF16),
            pltpu.VMEM((tk, D_HEAD), BF16),
            pltpu.VMEM((tk, 2 * tq), F32),
            pltpu.VMEM((tk, 2 * tq), F32),
            pltpu.VMEM((H_DIFF // 2, n_rows // 2, 2 * D_HEAD), BF16),
            pltpu.VMEM((PAGE, d_grp), F32),
            pltpu.VMEM((PAGE, d_grp), F32),
            pltpu.VMEM((n_rows, (1 + n_pp // n_grp) * PAGE), F32),
            pltpu.VMEM((n_grp, n_rows, 1), F32),
            pltpu.VMEM((n_grp, n_rows, 1), F32),
            pltpu.VMEM((n_grp, n_rows, D_HEAD), F32),
            pltpu.VMEM((n_slots, n_pp, page_cols, D_HEAD), F32),
            pltpu.VMEM((n_slots, n_pp, page_cols, D_HEAD), F32),
            pltpu.SemaphoreType.DMA((n_slots,)),
        ],
    )
    return pl.pallas_call(
        functools.partial(
            _diff_attention_kernel, nq=nq,
            prompt_kw=dict(tq=tq, tk=tk, cw=cw, lam_init=lam_init),
            sample_kw=dict(t=t, n_seq=n_dec, n_pp=n_pp, n_grp=n_grp, n_slots=n_slots, n_steps=n_steps, past=past,
                           lam_init=lam_init)),
        grid_spec=grid_spec,
        out_shape=[jax.ShapeDtypeStruct((batch * seq, d_grp), BF16),
                   jax.ShapeDtypeStruct((n_dec * t, d_grp), F32)],
        compiler_params=_cparams(("arbitrary",)),
        name="diff_attention",
    )(pt_flat, slopes, qd, kd16, vdt16, gd, qd_s, kd_s, vd_s, gd_s, gain, *lams, cache_k, cache_v)


def _finish_kernel(x_ref, yr_ref, yd_ref, w_ref, g_ref, o_ref):
    d_grp = yr_ref.shape[1]
    y = jnp.dot(yr_ref[...].astype(BF16), w_ref[:d_grp, :], preferred_element_type=F32)
    y = y + jnp.dot(yd_ref[...].astype(BF16), w_ref[d_grp:, :], preferred_element_type=F32)
    yn = y * lax.rsqrt(jnp.mean(y * y, axis=-1, keepdims=True) + NORM_EPS) * g_ref[...]
    o_ref[...] = x_ref[...] + yn


def _finish(x, y_ret, y_diff, w_bf16, norm_post, tm):
    n, d_model = x.shape
    d_grp = y_ret.shape[1]
    tok = lambda i: (i, 0)
    return pl.pallas_call(
        _finish_kernel,
        grid=(n // tm,),
        in_specs=[
            pl.BlockSpec((tm, d_model), tok),
            pl.BlockSpec((tm, d_grp), tok),
            pl.BlockSpec((tm, d_grp), tok),
            pl.BlockSpec((2 * d_grp, d_model), lambda i: (0, 0)),
            pl.BlockSpec((1, d_model), lambda i: (0, 0)),
        ],
        out_specs=pl.BlockSpec((tm, d_model), tok),
        out_shape=jax.ShapeDtypeStruct((n, d_model), F32),
        compiler_params=_cparams(("parallel",)),
        name="finish",
    )(x, y_ret, y_diff, w_bf16, norm_post.reshape(1, d_model))


def kernel(x_prompt, x_sample, cache_k, cache_v, page_table, state_ret, norm_pre, norm_post, w_in,
           ret_gn, diff_lq1, diff_lk1, diff_lq2, diff_lk2, diff_norm, w_out):
    batch, seq, d_model = x_prompt.shape
    n_dec, t_dec, _ = x_sample.shape
    depth = w_in.shape[0]
    assert depth == 1
    n_pages = page_table.shape[1]
    past = n_pages * PAGE
    layer = 0
    lam_init = 0.8 - 0.6 * math.exp(-0.3 * layer)

    lg = jnp.log(1.0 - jnp.exp2(-5.0 - jnp.arange(H_RET, dtype=F32)))
    assert 8 % H_DIFF == 0
    slope_exponents = -(8 // H_DIFF) * (jnp.arange(H_DIFF, dtype=jnp.int32) + 1)
    slopes = lax.bitcast_convert_type((slope_exponents + 127) << 23, F32)
    w_in16 = w_in[layer].astype(BF16)
    w_out16 = w_out[layer].astype(BF16)
    lams = [p[layer].reshape(1, DK_DIFF) for p in (diff_lq1, diff_lk1, diff_lq2, diff_lk2)]
    tm = 512

    xp = x_prompt.reshape(batch * seq, d_model)
    cos_p, sin_p = _rope_tables(jnp.arange(seq))
    qr, kr, vr, gr, qd, kd, vd, gd, kd16, vdt16 = _project(
        xp, norm_pre[layer], w_in16, cos_p, sin_p, tm, BF16, batch_seq=(batch, seq))
    xs = x_sample.reshape(n_dec * t_dec, d_model)
    cos_s, sin_s = _rope_tables(past + jnp.arange(tm) % t_dec)
    qr_s, kr_s, vr_s, gr_s, qd_s, kd_s, vd_s, gd_s, _, _ = _project(xs, norm_pre[layer], w_in16, cos_s, sin_s, tm, F32)

    y_ret, s_fin = _ret_prompt(lg, qr, kr, vr, gr, ret_gn[layer], batch, seq)
    y_ret_s, s_new = _ret_sample(lg, qr_s, kr_s, vr_s, gr_s, ret_gn[layer], state_ret[layer], n_dec, t_dec, n_seq=16)
    y_diff, y_diff_s = _diff_attention(slopes, qd, kd16, vdt16, gd, qd_s, kd_s, vd_s, gd_s, diff_norm[layer], lams,
                                       page_table, cache_k, cache_v, batch, seq, n_dec, t_dec, lam_init)
    out_p = _finish(xp, y_ret, y_diff, w_out16, norm_post[layer], tm)
    out_s = _finish(xs, y_ret_s, y_diff_s, w_out16, norm_post[layer], tm)

    return (
        out_p.reshape(batch, seq, d_model),
        out_s.reshape(n_dec, t_dec, d_model),
        kd,
        vd,
        s_fin.reshape(1, batch, H_RET, D_HEAD, D_HEAD),
        kd_s.reshape(1, n_dec, t_dec, H_DIFF, D_HEAD),
        vd_s.reshape(1, n_dec, t_dec, H_DIFF, D_HEAD),
        s_new.reshape(1, n_dec, H_RET, D_HEAD, D_HEAD),
    )
```

```python
import functools
import math

import jax
import jax.numpy as jnp
import numpy as np
from jax import lax
from jax.experimental import pallas as pl
from jax.experimental.pallas import tpu as pltpu

F32 = jnp.float32
BF16 = jnp.bfloat16

H_RET = 4
H_DIFF = 4
D_HEAD = 128
DK_DIFF = 64
RET_CHUNK = 128
PAGE = 128
ROPE_BASE = 10000.0
NORM_EPS = 1e-6
GN_EPS = 1e-5
NEG_INF = float("-inf")

VMEM_LIMIT = 56 * 1024 * 1024


def _cparams(sem):
    return pltpu.CompilerParams(dimension_semantics=sem, vmem_limit_bytes=VMEM_LIMIT)


def _silu(x):
    return x * (1.0 / (1.0 + jnp.exp(-x)))


def _nt(a, b):
    return lax.dot_general(a, b, (((1,), (1,)), ((), ())), preferred_element_type=F32)


def _tn(a, b):
    return lax.dot_general(a, b, (((0,), (0,)), ((), ())), preferred_element_type=F32)


def _proj_sections(x_ref, g_ref, w_ref):
    d_grp = H_RET * D_HEAD
    x = x_ref[...]
    ms = jnp.mean(x * x, axis=-1, keepdims=True)
    h = (x * lax.rsqrt(ms + NORM_EPS) * g_ref[...]).astype(BF16)

    def section(s):
        return jnp.dot(h, w_ref[:, s * d_grp:(s + 1) * d_grp], preferred_element_type=F32)

    return section


def _rope_heads(z, cos2, sin2, scale):
    out = []
    for hd in range(H_RET):
        zh = z[:, hd * D_HEAD:(hd + 1) * D_HEAD]
        rot = pltpu.roll(zh, D_HEAD // 2, axis=1)
        out.append((zh * cos2 + rot * sin2) * scale)
    return out


def _proj_diff_streams(section, qd_ref, kd_ref, vd_ref, gd_ref, kd16_ref, vdt16_ref):
    def store_heads(ref, z):
        if len(ref.shape) == 2:
            ref[...] = z
        else:
            for hd in range(H_DIFF):
                ref[:, hd, :] = z[:, hd * D_HEAD:(hd + 1) * D_HEAD]

    qd_ref[...] = (section(4) * (DK_DIFF ** -0.5)).astype(qd_ref.dtype)
    z = section(5)
    store_heads(kd_ref, z)
    kd16_ref[...] = z.astype(BF16)
    z = section(6)
    store_heads(vd_ref, z)
    vdt16_ref[...] = z.T.astype(BF16)
    gd_ref[...] = section(7)


def _proj_kernel(x_ref, g_ref, w_ref, cos_ref, sin_ref,
                 qr_ref, kr_ref, vr_ref, gr_ref, qd_ref, kd_ref, vd_ref, gd_ref, kd16_ref, vdt16_ref):
    section = _proj_sections(x_ref, g_ref, w_ref)
    cos2 = cos_ref[...]
    sin2 = sin_ref[...]
    for hd, r in enumerate(_rope_heads(section(0), cos2, sin2, 1.0)):
        qr_ref[:, hd * D_HEAD:(hd + 1) * D_HEAD] = r.astype(qr_ref.dtype)
    for hd, r in enumerate(_rope_heads(section(1), cos2, sin2, D_HEAD ** -0.5)):
        kr_ref[:, hd * D_HEAD:(hd + 1) * D_HEAD] = r.astype(kr_ref.dtype)
    vr_ref[...] = section(2).astype(vr_ref.dtype)
    gr_ref[...] = section(3)
    _proj_diff_streams(section, qd_ref, kd_ref, vd_ref, gd_ref, kd16_ref, vdt16_ref)


def _project(x, norm_pre, w_bf16, cos2, sin2, tm, ret_dtype, batch_seq=None):
    n, d_model = x.shape
    d_in = w_bf16.shape[1]
    d_grp = d_in // 8
    n_tab = cos2.shape[0] // tm
    tok = lambda i: (i, 0)
    out_blk = pl.BlockSpec((tm, d_grp), tok)
    if batch_seq is None:
        kv_shape = jax.ShapeDtypeStruct((n, d_grp), F32)
        kv_blk = out_blk
    else:
        batch, seq = batch_seq
        per_seq = seq // tm
        kv_shape = jax.ShapeDtypeStruct((1, batch, seq, H_DIFF, D_HEAD), F32)
        kv_blk = pl.BlockSpec((None, None, tm, H_DIFF, D_HEAD), lambda i: (0, i // per_seq, i % per_seq, 0, 0))
    shapes = [
        jax.ShapeDtypeStruct((n, d_grp), ret_dtype),
        jax.ShapeDtypeStruct((n, d_grp), ret_dtype),
        jax.ShapeDtypeStruct((n, d_grp), ret_dtype),
        jax.ShapeDtypeStruct((n, d_grp), F32),
        jax.ShapeDtypeStruct((n, d_grp), ret_dtype),
        kv_shape,
        kv_shape,
        jax.ShapeDtypeStruct((n, d_grp), F32),
        jax.ShapeDtypeStruct((n, d_grp), BF16),
        jax.ShapeDtypeStruct((d_grp, n), BF16),
    ]
    out_specs = [out_blk] * 5 + [kv_blk, kv_blk, out_blk, out_blk, pl.BlockSpec((d_grp, tm), lambda i: (0, i))]
    return pl.pallas_call(
        _proj_kernel,
        grid=(n // tm,),
        in_specs=[
            pl.BlockSpec((tm, d_model), tok),
            pl.BlockSpec((1, d_model), lambda i: (0, 0)),
            pl.BlockSpec((d_model, d_in), lambda i: (0, 0)),
            pl.BlockSpec((tm, D_HEAD), lambda i: (i % n_tab, 0)),
            pl.BlockSpec((tm, D_HEAD), lambda i: (i % n_tab, 0)),
        ],
        out_specs=out_specs,
        out_shape=shapes,
        compiler_params=_cparams(("parallel",)),
        name="proj",
    )(x, norm_pre.reshape(1, d_model), w_bf16, cos2, sin2)


def _rope_tables(pos):
    half = D_HEAD // 2
    inv = ROPE_BASE ** (-np.arange(half, dtype=np.float64) / half)
    ang = np.asarray(pos, np.float64)[:, None] * inv[None, :]
    cos, sin = np.cos(ang).astype(np.float32), np.sin(ang).astype(np.float32)
    return jnp.asarray(np.concatenate([cos, cos], axis=-1)), jnp.asarray(np.concatenate([-sin, sin], axis=-1))


def _group_norm_gate(o, gain, gate):
    mu = jnp.mean(o, axis=-1, keepdims=True)
    d = o - mu
    var = jnp.mean(d * d, axis=-1, keepdims=True)
    return d * lax.rsqrt(var + GN_EPS) * gain * _silu(gate)


def _decay_terms(n_rows, chunk, lg):
    row = lax.broadcasted_iota(jnp.int32, (n_rows, n_rows), 0)
    col = lax.broadcasted_iota(jnp.int32, (n_rows, n_rows), 1)
    dist = (row - col).astype(F32)
    dmask = jnp.where(col <= row, jnp.exp(jnp.maximum(dist, 0.0) * lg), 0.0)
    if n_rows != chunk:
        dmask = jnp.where(row // chunk == col // chunk, dmask, 0.0)
    idx = (lax.broadcasted_iota(jnp.int32, (n_rows, 1), 0) % chunk).astype(F32)
    xi = jnp.exp((idx + 1.0) * lg)
    zeta = jnp.exp((chunk - 1.0 - idx) * lg)
    g_chunk = jnp.exp(jnp.full((1, D_HEAD), chunk, F32) * lg)
    return dmask, xi, zeta, g_chunk


def _ret_prompt_kernel(lg_ref, q_ref, k_ref, v_ref, gate_ref, gn_ref, y_ref, s_out_ref,
                       s_scr, dmask_scr, xi_scr, zeta_scr, gch_scr, *, n_ck):
    c = pl.program_id(1)

    @pl.when(c == 0)
    def _():
        s_scr[...] = jnp.zeros_like(s_scr)
        for h in range(H_RET):
            dmask, xi, zeta, g_chunk = _decay_terms(RET_CHUNK, RET_CHUNK, lg_ref[h])
            dmask_scr[h] = dmask
            xi_scr[h] = jnp.broadcast_to(xi, (RET_CHUNK, D_HEAD))
            zeta_scr[h] = jnp.broadcast_to(zeta, (RET_CHUNK, D_HEAD))
            gch_scr[h] = jnp.broadcast_to(g_chunk, (8, D_HEAD))

    for h in range(H_RET):
        sl = slice(h * D_HEAD, (h + 1) * D_HEAD)
        s = s_scr[h]
        for a in range(n_ck):
            rows = slice(a * RET_CHUNK, (a + 1) * RET_CHUNK)
            q = q_ref[rows, sl]
            k = k_ref[rows, sl]
            v = v_ref[rows, sl]
            inner = _nt(q, k) * dmask_scr[h]
            o = jnp.dot(inner.astype(BF16), v, preferred_element_type=F32)
            o = o + jnp.dot(q, s.astype(BF16), preferred_element_type=F32) * xi_scr[h]
            kz = (k.astype(F32) * zeta_scr[h]).astype(BF16)
            s = s * gch_scr[h, 0:1, :] + _tn(kz, v)
            y_ref[rows, sl] = _group_norm_gate(o, gn_ref[h:h + 1, :], gate_ref[rows, sl]).astype(y_ref.dtype)
        s_scr[h] = s

    @pl.when(c == pl.num_programs(1) - 1)
    def _():
        s_out_ref[...] = s_scr[...]


def _ret_prompt(lg, qr, kr, vr, gr, gn, batch, seq, n_ck=4):
    n_chunk = seq // (RET_CHUNK * n_ck)
    d_grp = H_RET * D_HEAD
    blk = pl.BlockSpec((RET_CHUNK * n_ck, d_grp), lambda b, c: (b * n_chunk + c, 0))
    tile = pltpu.VMEM((H_RET, RET_CHUNK, D_HEAD), F32)
    return pl.pallas_call(
        functools.partial(_ret_prompt_kernel, n_ck=n_ck),
        grid=(batch, n_chunk),
        in_specs=[
            pl.BlockSpec(memory_space=pltpu.SMEM),
            blk, blk, blk, blk,
            pl.BlockSpec((H_RET, D_HEAD), lambda b, c: (0, 0)),
        ],
        out_specs=[
            blk,
            pl.BlockSpec((None, H_RET, D_HEAD, D_HEAD), lambda b, c: (b, 0, 0, 0)),
        ],
        out_shape=[
            jax.ShapeDtypeStruct((batch * seq, d_grp), BF16),
            jax.ShapeDtypeStruct((batch, H_RET, D_HEAD, D_HEAD), F32),
        ],
        scratch_shapes=[pltpu.VMEM((H_RET, D_HEAD, D_HEAD), F32), tile, tile, tile,
                        pltpu.VMEM((H_RET, 8, D_HEAD), F32)],
        compiler_params=_cparams(("parallel", "arbitrary")),
        name="ret_prompt",
    )(lg, qr, kr, vr, gr, gn)


def _ret_sample_kernel(lg_ref, q_ref, k_ref, v_ref, gate_ref, gn_ref, s_ref, y_ref, s_out_ref, *, n_seq, t):
    n_rows = n_seq * t
    seq_of_row = lax.broadcasted_iota(jnp.int32, (n_rows, 1), 0) // t
    for h in range(H_RET):
        sl = slice(h * D_HEAD, (h + 1) * D_HEAD)
        lg = lg_ref[h]
        dmask, xi, zeta, g_chunk = _decay_terms(n_rows, t, lg)
        qf = q_ref[:, sl]
        q = qf.astype(BF16)
        v = v_ref[:, sl].astype(BF16)
        kf = k_ref[:, sl]
        inner = _nt(q, kf.astype(BF16)) * dmask
        o = jnp.dot(inner.astype(BF16), v, preferred_element_type=F32)
        kz = kf * zeta
        cross = []
        for b in range(n_seq):
            s_old = s_ref[b, h]
            q_b = qf[b * t:(b + 1) * t, :].astype(BF16)
            cross.append(jnp.dot(q_b, s_old.astype(BF16), preferred_element_type=F32))
            kz_b = jnp.where(seq_of_row == b, kz, 0.0).astype(BF16)
            s_out_ref[b, h] = s_old * g_chunk + _tn(kz_b, v)
        o = o + jnp.concatenate(cross, axis=0) * xi
        y_ref[:, sl] = _group_norm_gate(o, gn_ref[h:h + 1, :], gate_ref[:, sl]).astype(y_ref.dtype)


def _ret_sample(lg, qr, kr, vr, gr, gn, state, n_batch, t, n_seq):
    d_grp = H_RET * D_HEAD
    blk = pl.BlockSpec((n_seq * t, d_grp), lambda i: (i, 0))
    sblk = pl.BlockSpec((n_seq, H_RET, D_HEAD, D_HEAD), lambda i: (i, 0, 0, 0))
    return pl.pallas_call(
        functools.partial(_ret_sample_kernel, n_seq=n_seq, t=t),
        grid=(n_batch // n_seq,),
        in_specs=[
            pl.BlockSpec(memory_space=pltpu.SMEM),
            blk, blk, blk, blk,
            pl.BlockSpec((H_RET, D_HEAD), lambda i: (0, 0)),
            sblk,
        ],
        out_specs=[blk, sblk],
        out_shape=[
            jax.ShapeDtypeStruct((n_batch * t, d_grp), BF16),
            jax.ShapeDtypeStruct((n_batch, H_RET, D_HEAD, D_HEAD), F32),
        ],
        compiler_params=_cparams(("parallel",)),
        name="ret_sample",
    )(lg, qr, kr, vr, gr, gn, state)


def _lambda(lq1_ref, lk1_ref, lq2_ref, lk2_ref, lam_init):
    a = jnp.sum(lq1_ref[...] * lk1_ref[...], axis=-1, keepdims=True)
    b = jnp.sum(lq2_ref[...] * lk2_ref[...], axis=-1, keepdims=True)
    return jnp.exp(a) - jnp.exp(b) + lam_init


def _diff_finish(o, gain, gate, lam_init):
    y = o * lax.rsqrt(jnp.mean(o * o, axis=-1, keepdims=True) + NORM_EPS)
    return y * gain * (1.0 - lam_init) * _silu(gate)


def _map_split(q):
    lane = lax.broadcasted_iota(jnp.int32, q.shape, 1)
    zero = jnp.zeros_like(q)
    return jnp.concatenate([jnp.where(lane < DK_DIFF, q, zero), jnp.where(lane >= DK_DIFF, q, zero)], axis=0)


def _diff_prompt_tile(h, qi, lam, slope_ref, q_ref, k_ref, vt_ref, gate_ref, gain_ref, y_ref,
                      m_ref, l_ref, acc_ref, qbd_ref, kaux_ref, sa_ref, sb_ref, *, tq, tk, cw, lam_init):
    slope = slope_ref[h]
    lane_q = lax.broadcasted_iota(jnp.int32, (2 * tq, D_HEAD), 1)
    qbd_ref[:, :D_HEAD] = _map_split(q_ref[...])
    qbd_ref[:, D_HEAD:] = jnp.where(lane_q < 2, 1.0, 0.0).astype(BF16)
    row_k = lax.broadcasted_iota(jnp.int32, (tk, D_HEAD), 0)
    lane_k = lax.broadcasted_iota(jnp.int32, (tk, D_HEAD), 1)
    j_part = jnp.where(lane_k == 0, (row_k // 16) * 16, jnp.where(lane_k == 1, row_k % 16, 0))
    kaux_ref[...] = (j_part.astype(F32) * slope).astype(BF16)

    m_ref[...] = jnp.full_like(m_ref, NEG_INF)
    l_ref[...] = jnp.zeros_like(l_ref)
    acc_ref[...] = jnp.zeros_like(acc_ref)

    def scores(ki, s_ref):
        start = pl.multiple_of(ki * tk, tk)
        k_aug = jnp.concatenate([k_ref[pl.ds(start, tk), :], kaux_ref[...]], axis=1)
        s_ref[...] = _nt(k_aug, qbd_ref[...])

    def absorb(ki, s_ref, diag_off):
        start = pl.multiple_of(ki * tk, tk)
        vt = vt_ref[:, pl.ds(start, tk)]
        if diag_off is not None:
            kml = lax.broadcasted_iota(jnp.int32, (tk, cw), 0) - lax.broadcasted_iota(jnp.int32, (tk, cw), 1)
        for c in range(2 * tq // cw):
            cols = slice(c * cw, (c + 1) * cw)
            q0 = (c * cw) % tq
            if diag_off is not None and diag_off >= q0 + cw:
                continue
            s = s_ref[:, cols]
            if diag_off is not None:
                s = jnp.where(kml <= q0 - diag_off, s, NEG_INF)
            off = slope * (qi * tq - ki * tk).astype(F32)
            m = m_ref[:, cols]
            m_new = jnp.maximum(m, jnp.max(s, axis=0, keepdims=True) - off)
            corr = jnp.exp(m - m_new)
            p = jnp.exp(s - (m_new + off))
            l_ref[:, cols] = l_ref[:, cols] * corr + jnp.sum(p, axis=0, keepdims=True)
            acc_ref[:, cols] = acc_ref[:, cols] * corr + jnp.dot(vt, p.astype(BF16), preferred_element_type=F32)
            m_ref[:, cols] = m_new

    assert tq == 2 * tk
    n_full = 2 * qi
    scores(0, sa_ref)

    def body(pi, carry):
        scores(2 * pi + 1, sb_ref)
        absorb(2 * pi, sa_ref, None)
        scores(2 * pi + 2, sa_ref)
        absorb(2 * pi + 1, sb_ref, None)
        return carry

    def run_pairs(lo, hi):
        lax.fori_loop(lo, hi, body, 0)

    def finish():
        scores(n_full + 1, sb_ref)
        absorb(n_full, sa_ref, 0)
        absorb(n_full + 1, sb_ref, tk)
        o_t = acc_ref[...] / l_ref[...]
        o = (o_t[:, :tq] - lam * o_t[:, tq:]).T
        y_ref[...] = _diff_finish(o, gain_ref[pl.ds(h, 1), :], gate_ref[...], lam_init).astype(y_ref.dtype)

    return run_pairs, finish


def _diff_sample_sequence(seq, lam, pt_ref, q_ref, kl_ref, vl_ref, gate_ref, gain_ref, ck_hbm, cv_hbm, y_ref,
                          qpair_ref, kpad_ref, vpad_ref, bias_ref, m_ref, l_ref, acc_ref, kbuf, vbuf, sem,
                          *, t, n_seq, n_pp, n_grp, n_slots, n_steps, past, lam_init):
    n_total = n_seq * n_steps
    rows_h = 2 * t
    n_rows = H_DIFF * rows_h
    ppg = n_pp // n_grp

    def batch_copies(src_step, slot, known_pages):
        for i in range(n_pp):
            page = pt_ref[src_step * n_pp + i] if known_pages else 0
            yield pltpu.make_async_copy(ck_hbm.at[page], kbuf.at[slot, i], sem.at[slot])
            yield pltpu.make_async_copy(cv_hbm.at[page], vbuf.at[slot, i], sem.at[slot])

    def start_batch(src_step):
        for cp in batch_copies(src_step, src_step % n_slots, True):
            cp.start()

    @pl.when(seq == 0)
    def _():
        for d in range(n_slots - 1):
            start_batch(d)

    def alibi(width, causal):
        rowi = lax.broadcasted_iota(jnp.int32, (n_rows, width), 0)
        tok_k = lax.broadcasted_iota(jnp.int32, (n_rows, width), 1)
        slope = jnp.exp2(-8.0 * ((rowi // rows_h).astype(F32) + 1.0) / H_DIFF)
        tok_q = rowi % t
        bias = -slope * (tok_q - tok_k).astype(F32)
        if causal:
            bias = jnp.where(tok_k <= tok_q, bias, NEG_INF)
        return bias

    def update(g, k_heads, v_heads, bias, shift):
        pairs = range(0, H_DIFF, 2)
        sa = jnp.concatenate(
            [_nt(qpair_ref[h // 2], jnp.concatenate([k_heads[h], k_heads[h + 1]], axis=1)) for h in pairs],
            axis=0) + bias
        m = m_ref[g]
        m_new = jnp.maximum(m, jnp.max(sa, axis=-1, keepdims=True) - shift)
        corr = jnp.exp(m - m_new)
        p = jnp.exp(sa - (m_new + shift))
        l_ref[g] = l_ref[g] * corr + jnp.sum(p, axis=-1, keepdims=True)
        pv = []
        for h in pairs:
            both = jnp.dot(p[h * rows_h:(h + 2) * rows_h, :].astype(BF16),
                           jnp.concatenate([v_heads[h], v_heads[h + 1]], axis=1), preferred_element_type=F32)
            pv += [both[:rows_h, :D_HEAD], both[rows_h:, D_HEAD:]]
        acc_ref[g] = acc_ref[g] * corr + jnp.concatenate(pv, axis=0)
        m_ref[g] = m_new

    rowi = lax.broadcasted_iota(jnp.int32, (n_rows, 1), 0)
    slope = jnp.exp2(-8.0 * ((rowi // rows_h).astype(F32) + 1.0) / H_DIFF)

    @pl.when(seq == 0)
    def _():
        bias_ref[:, :PAGE] = alibi(PAGE, causal=True) + slope * float(past)
        bias_ref[:, PAGE:] = alibi(ppg * PAGE, causal=False)
        kpad_ref[...] = jnp.zeros_like(kpad_ref)
        vpad_ref[...] = jnp.zeros_like(vpad_ref)

    for h in range(0, H_DIFF, 2):
        q0 = _map_split(q_ref[:, h * D_HEAD:(h + 1) * D_HEAD])
        q1 = _map_split(q_ref[:, (h + 1) * D_HEAD:(h + 2) * D_HEAD])
        zero = jnp.zeros_like(q0)
        qpair_ref[h // 2] = jnp.concatenate(
            [jnp.concatenate([q0, zero], axis=1), jnp.concatenate([zero, q1], axis=1)], axis=0).astype(BF16)
    kpad_ref[0:t, :] = kl_ref[...]
    vpad_ref[0:t, :] = vl_ref[...]
    m_ref[...] = jnp.full_like(m_ref, NEG_INF)
    l_ref[...] = jnp.zeros_like(l_ref)
    acc_ref[...] = jnp.zeros_like(acc_ref)

    def head_rows(ref, h):
        return ref[pl.ds(h, PAGE, stride=H_DIFF), :].astype(BF16)

    def new_token_rows(ref, h):
        return ref[:, h * D_HEAD:(h + 1) * D_HEAD].astype(BF16)

    def run_step(j):
        step = seq * n_steps + j

        @pl.when(step + (n_slots - 1) < n_total)
        def _():
            start_batch(step + (n_slots - 1))

        slot = step % n_slots
        for cp in batch_copies(step, slot, False):
            cp.wait()
        for g in range(n_grp):
            pages = range(g * ppg, (g + 1) * ppg)
            first_key = (j * n_pp + g * ppg) * PAGE
            shift = slope * float(past - first_key)
            first = j == 0 and g == 0
            k_heads = [jnp.concatenate(([new_token_rows(kpad_ref, h)] if first else [])
                                       + [head_rows(kbuf.at[slot, i], h) for i in pages], axis=0)
                       for h in range(H_DIFF)]
            v_heads = [jnp.concatenate(([new_token_rows(vpad_ref, h)] if first else [])
                                       + [head_rows(vbuf.at[slot, i], h) for i in pages], axis=0)
                       for h in range(H_DIFF)]
            update(g, k_heads, v_heads, bias_ref[...] if first else bias_ref[:, PAGE:], shift)

    def finish():
        m_all = m_ref[0]
        for g in range(1, n_grp):
            m_all = jnp.maximum(m_all, m_ref[g])
        l_all = jnp.zeros_like(m_all)
        acc = jnp.zeros((n_rows, D_HEAD), F32)
        for g in range(n_grp):
            w = jnp.exp(m_ref[g] - m_all)
            l_all = l_all + l_ref[g] * w
            acc = acc + acc_ref[g] * w
        o_all = acc / l_all
        for h in range(H_DIFF):
            sl = slice(h * D_HEAD, (h + 1) * D_HEAD)
            o = o_all[h * rows_h:h * rows_h + t, :] - lam * o_all[h * rows_h + t:(h + 1) * rows_h, :]
            y_ref[:, sl] = _diff_finish(o, gain_ref[h:h + 1, :], gate_ref[:, sl], lam_init).astype(y_ref.dtype)

    return run_step, finish


def _diff_attention_kernel(pt_ref, slope_ref, qp_ref, kp_ref, vtp_ref, gatep_ref,
                           qs_ref, kls_ref, vls_ref, gates_ref, gain_ref,
                           lq1_ref, lk1_ref, lq2_ref, lk2_ref, ck_hbm, cv_hbm, yp_ref, ys_ref,
                           m_ref, l_ref, acc_ref, qbd_ref, kaux_ref, sa_ref, sb_ref,
                           qpair_ref, kpad_ref, vpad_ref, bias_ref, ms_ref, ls_ref, accs_ref, kbuf, vbuf, sem,
                           *, nq, prompt_kw, sample_kw):
    g = pl.program_id(0)
    lam = _lambda(lq1_ref, lk1_ref, lq2_ref, lk2_ref, prompt_kw["lam_init"])
    page_step, finish_sequence = _diff_sample_sequence(
        g, lam, pt_ref, qs_ref, kls_ref, vls_ref, gates_ref, gain_ref, ck_hbm, cv_hbm, ys_ref,
        qpair_ref, kpad_ref, vpad_ref, bias_ref, ms_ref, ls_ref, accs_ref, kbuf, vbuf, sem, **sample_kw)
    h = (g // nq) % H_DIFF
    qi = _tile_order(g % nq, nq)
    key_pairs, finish_tile = _diff_prompt_tile(
        h, qi, lam, slope_ref, qp_ref, kp_ref, vtp_ref, gatep_ref, gain_ref, yp_ref,
        m_ref, l_ref, acc_ref, qbd_ref, kaux_ref, sa_ref, sb_ref, **prompt_kw)
    n_steps = sample_kw["n_steps"]
    for j in range(n_steps // 2):
        page_step(j)
    key_pairs(0, qi // 2)
    for j in range(n_steps // 2, n_steps):
        page_step(j)
    key_pairs(qi // 2, qi)
    finish_tile()
    finish_sequence()


def _tile_order(i, nq):
    return jnp.where(i % 2 == 0, nq - 1 - i // 2, i // 2)


def _diff_attention(slopes, qd, kd16, vdt16, gd, qd_s, kd_s, vd_s, gd_s, gain, lams, page_table, cache_k, cache_v,
                    batch, seq, n_dec, t, lam_init, tq=512, tk=256, cw=256, n_pp=16, n_grp=1, n_slots=5):
    assert tq % cw == 0
    nq = seq // tq
    n_pages = page_table.shape[1]
    past = n_pages * PAGE
    d_grp = H_DIFF * D_HEAD
    n_rows = H_DIFF * 2 * t
    n_steps = n_pages // n_pp
    assert n_dec == batch * H_DIFF * nq and n_pages % n_pp == 0 and n_dec * n_steps >= n_slots - 1
    pt_flat = page_table.reshape(-1)
    page_cols = PAGE * H_DIFF
    n_pool = cache_k.shape[1]
    cache_k = cache_k.reshape(n_pool, page_cols, D_HEAD)
    cache_v = cache_v.reshape(n_pool, page_cols, D_HEAD)

    def tile(g):
        return g // (H_DIFF * nq), (g // nq) % H_DIFF, _tile_order(g % nq, nq)

    def q_map(g, pt):
        b, h, qi = tile(g)
        return b * nq + qi, h

    qblk = pl.BlockSpec((tq, D_HEAD), q_map)
    tokblk = pl.BlockSpec((t, d_grp), lambda g, pt: (g, 0))
    vec = pl.BlockSpec((1, DK_DIFF), lambda g, pt: (0, 0))
    hbm = pl.BlockSpec(memory_space=pl.ANY)
    grid_spec = pltpu.PrefetchScalarGridSpec(
        num_scalar_prefetch=1,
        grid=(n_dec,),
        in_specs=[
            pl.BlockSpec(memory_space=pltpu.SMEM),
            qblk,
            pl.BlockSpec((seq, D_HEAD), lambda g, pt: (tile(g)[0], tile(g)[1])),
            pl.BlockSpec((D_HEAD, seq), lambda g, pt: (tile(g)[1], tile(g)[0])),
            qblk,
            tokblk, tokblk, tokblk, tokblk,
            pl.BlockSpec((H_DIFF, D_HEAD), lambda g, pt: (0, 0)),
            vec, vec, vec, vec, hbm, hbm,
        ],
        out_specs=[qblk, tokblk],
        scratch_shapes=[
            pltpu.VMEM((1, 2 * tq), F32),
            pltpu.VMEM((1, 2 * tq), F32),
            pltpu.VMEM((D_HEAD, 2 * tq), F32),
            pltpu.VMEM((2 * tq, 2 * D_HEAD), BF16),
            pltpu.VMEM((tk, D_HEAD), BF16),
            pltpu.VMEM((tk, 2 * tq), F32),
            pltpu.VMEM((tk, 2 * tq), F32),
            pltpu.VMEM((H_DIFF // 2, n_rows // 2, 2 * D_HEAD), BF16),
            pltpu.VMEM((PAGE, d_grp), F32),
            pltpu.VMEM((PAGE, d_grp), F32),
            pltpu.VMEM((n_rows, (1 + n_pp // n_grp) * PAGE), F32),
            pltpu.VMEM((n_grp, n_rows, 1), F32),
            pltpu.VMEM((n_grp, n_rows, 1), F32),
            pltpu.VMEM((n_grp, n_rows, D_HEAD), F32),
            pltpu.VMEM((n_slots, n_pp, page_cols, D_HEAD), F32),
            pltpu.VMEM((n_slots, n_pp, page_cols, D_HEAD), F32),
            pltpu.SemaphoreType.DMA((n_slots,)),
        ],
    )
    return pl.pallas_call(
        functools.partial(
            _diff_attention_kernel, nq=nq,
            prompt_kw=dict(tq=tq, tk=tk, cw=cw, lam_init=lam_init),
            sample_kw=dict(t=t, n_seq=n_dec, n_pp=n_pp, n_grp=n_grp, n_slots=n_slots, n_steps=n_steps, past=past,
                           lam_init=lam_init)),
        grid_spec=grid_spec,
        out_shape=[jax.ShapeDtypeStruct((batch * seq, d_grp), BF16),
                   jax.ShapeDtypeStruct((n_dec * t, d_grp), F32)],
        compiler_params=_cparams(("arbitrary",)),
        name="diff_attention",
    )(pt_flat, slopes, qd, kd16, vdt16, gd, qd_s, kd_s, vd_s, gd_s, gain, *lams, cache_k, cache_v)


def _finish_kernel(x_ref, yr_ref, yd_ref, w_ref, g_ref, o_ref):
    d_grp = yr_ref.shape[1]
    y = jnp.dot(yr_ref[...].astype(BF16), w_ref[:d_grp, :], preferred_element_type=F32)
    y = y + jnp.dot(yd_ref[...].astype(BF16), w_ref[d_grp:, :], preferred_element_type=F32)
    yn = y * lax.rsqrt(jnp.mean(y * y, axis=-1, keepdims=True) + NORM_EPS) * g_ref[...]
    o_ref[...] = x_ref[...] + yn


def _finish(x, y_ret, y_diff, w_bf16, norm_post, tm):
    n, d_model = x.shape
    d_grp = y_ret.shape[1]
    tok = lambda i: (i, 0)
    return pl.pallas_call(
        _finish_kernel,
        grid=(n // tm,),
        in_specs=[
            pl.BlockSpec((tm, d_model), tok),
            pl.BlockSpec((tm, d_grp), tok),
            pl.BlockSpec((tm, d_grp), tok),
            pl.BlockSpec((2 * d_grp, d_model), lambda i: (0, 0)),
            pl.BlockSpec((1, d_model), lambda i: (0, 0)),
        ],
        out_specs=pl.BlockSpec((tm, d_model), tok),
        out_shape=jax.ShapeDtypeStruct((n, d_model), F32),
        compiler_params=_cparams(("parallel",)),
        name="finish",
    )(x, y_ret, y_diff, w_bf16, norm_post.reshape(1, d_model))


def kernel(x_prompt, x_sample, cache_k, cache_v, page_table, state_ret, norm_pre, norm_post, w_in,
           ret_gn, diff_lq1, diff_lk1, diff_lq2, diff_lk2, diff_norm, w_out):
    batch, seq, d_model = x_prompt.shape
    n_dec, t_dec, _ = x_sample.shape
    depth = w_in.shape[0]
    assert depth == 1
    n_pages = page_table.shape[1]
    past = n_pages * PAGE
    layer = 0
    lam_init = 0.8 - 0.6 * math.exp(-0.3 * layer)

    lg = jnp.log(1.0 - jnp.exp2(-5.0 - jnp.arange(H_RET, dtype=F32)))
    assert 8 % H_DIFF == 0
    slope_exponents = -(8 // H_DIFF) * (jnp.arange(H_DIFF, dtype=jnp.int32) + 1)
    slopes = lax.bitcast_convert_type((slope_exponents + 127) << 23, F32)
    w_in16 = w_in[layer].astype(BF16)
    w_out16 = w_out[layer].astype(BF16)
    lams = [p[layer].reshape(1, DK_DIFF) for p in (diff_lq1, diff_lk1, diff_lq2, diff_lk2)]
    tm = 512

    xp = x_prompt.reshape(batch * seq, d_model)
    cos_p, sin_p = _rope_tables(np.arange(seq))
    qr, kr, vr, gr, qd, kd, vd, gd, kd16, vdt16 = _project(
        xp, norm_pre[layer], w_in16, cos_p, sin_p, tm, BF16, batch_seq=(batch, seq))
    xs = x_sample.reshape(n_dec * t_dec, d_model)
    cos_s, sin_s = _rope_tables(past + np.arange(tm) % t_dec)
    qr_s, kr_s, vr_s, gr_s, qd_s, kd_s, vd_s, gd_s, _, _ = _project(xs, norm_pre[layer], w_in16, cos_s, sin_s, tm, F32)

    y_ret, s_fin = _ret_prompt(lg, qr, kr, vr, gr, ret_gn[layer], batch, seq)
    y_ret_s, s_new = _ret_sample(lg, qr_s, kr_s, vr_s, gr_s, ret_gn[layer], state_ret[layer], n_dec, t_dec, n_seq=16)
    y_diff, y_diff_s = _diff_attention(slopes, qd, kd16, vdt16, gd, qd_s, kd_s, vd_s, gd_s, diff_norm[layer], lams,
                                       page_table, cache_k, cache_v, batch, seq, n_dec, t_dec, lam_init)
    out_p = _finish(xp, y_ret, y_diff, w_out16, norm_post[layer], tm)
    out_s = _finish(xs, y_ret_s, y_diff_s, w_out16, norm_post[layer], tm)

    return (
        out_p.reshape(batch, seq, d_model),
        out_s.reshape(n_dec, t_dec, d_model),
        kd,
        vd,
        s_fin.reshape(1, batch, H_RET, D_HEAD, D_HEAD),
        kd_s.reshape(1, n_dec, t_dec, H_DIFF, D_HEAD),
        vd_s.reshape(1, n_dec, t_dec, H_DIFF, D_HEAD),
        s_new.reshape(1, n_dec, H_RET, D_HEAD, D_HEAD),
    )
```

```python
import functools
import math

import jax
import jax.numpy as jnp
import numpy as np
from jax import lax
from jax.experimental import pallas as pl
from jax.experimental.pallas import tpu as pltpu

F32 = jnp.float32
BF16 = jnp.bfloat16

H_RET = 4
H_DIFF = 4
D_HEAD = 128
DK_DIFF = 64
RET_CHUNK = 128
PAGE = 128
ROPE_BASE = 10000.0
NORM_EPS = 1e-6
GN_EPS = 1e-5
NEG_INF = float("-inf")

VMEM_LIMIT = 61 * 1024 * 1024


def _cparams(sem):
    return pltpu.CompilerParams(dimension_semantics=sem, vmem_limit_bytes=VMEM_LIMIT)


def _silu(x):
    return x * (1.0 / (1.0 + jnp.exp(-x)))


def _nt(a, b):
    return lax.dot_general(a, b, (((1,), (1,)), ((), ())), preferred_element_type=F32)


def _tn(a, b):
    return lax.dot_general(a, b, (((0,), (0,)), ((), ())), preferred_element_type=F32)


def _proj_sections(x_ref, g_ref, w_ref):
    d_grp = H_RET * D_HEAD
    x = x_ref[...]
    ms = jnp.mean(x * x, axis=-1, keepdims=True)
    h = (x * lax.rsqrt(ms + NORM_EPS) * g_ref[...]).astype(BF16)

    def section(s):
        return jnp.dot(h, w_ref[:, s * d_grp:(s + 1) * d_grp], preferred_element_type=F32)

    return section


def _rope_heads(z, cos2, sin2, scale):
    out = []
    for hd in range(H_RET):
        zh = z[:, hd * D_HEAD:(hd + 1) * D_HEAD]
        rot = pltpu.roll(zh, D_HEAD // 2, axis=1)
        out.append((zh * cos2 + rot * sin2) * scale)
    return out


def _proj_diff_streams(section, qd_ref, kd_ref, vd_ref, gd_ref, kd16_ref, vdt16_ref):
    def store_heads(ref, z):
        if len(ref.shape) == 2:
            ref[...] = z
        else:
            for hd in range(H_DIFF):
                ref[:, hd, :] = z[:, hd * D_HEAD:(hd + 1) * D_HEAD]

    qd_ref[...] = (section(4) * (DK_DIFF ** -0.5)).astype(qd_ref.dtype)
    z = section(5)
    store_heads(kd_ref, z)
    kd16_ref[...] = z.astype(BF16)
    z = section(6)
    store_heads(vd_ref, z)
    vdt16_ref[...] = z.T.astype(BF16)
    gd_ref[...] = section(7)


def _proj_kernel(x_ref, g_ref, w_ref, cos_ref, sin_ref,
                 qr_ref, kr_ref, vr_ref, gr_ref, qd_ref, kd_ref, vd_ref, gd_ref, kd16_ref, vdt16_ref):
    section = _proj_sections(x_ref, g_ref, w_ref)
    cos2 = cos_ref[...]
    sin2 = sin_ref[...]
    for hd, r in enumerate(_rope_heads(section(0), cos2, sin2, 1.0)):
        qr_ref[:, hd * D_HEAD:(hd + 1) * D_HEAD] = r.astype(qr_ref.dtype)
    for hd, r in enumerate(_rope_heads(section(1), cos2, sin2, D_HEAD ** -0.5)):
        kr_ref[:, hd * D_HEAD:(hd + 1) * D_HEAD] = r.astype(kr_ref.dtype)
    vr_ref[...] = section(2).astype(vr_ref.dtype)
    gr_ref[...] = section(3)
    _proj_diff_streams(section, qd_ref, kd_ref, vd_ref, gd_ref, kd16_ref, vdt16_ref)


def _project(x, norm_pre, w_bf16, cos2, sin2, tm, ret_dtype, batch_seq=None):
    n, d_model = x.shape
    d_in = w_bf16.shape[1]
    d_grp = d_in // 8
    n_tab = cos2.shape[0] // tm
    tok = lambda i: (i, 0)
    out_blk = pl.BlockSpec((tm, d_grp), tok)
    if batch_seq is None:
        kv_shape = jax.ShapeDtypeStruct((n, d_grp), F32)
        kv_blk = out_blk
    else:
        batch, seq = batch_seq
        per_seq = seq // tm
        kv_shape = jax.ShapeDtypeStruct((1, batch, seq, H_DIFF, D_HEAD), F32)
        kv_blk = pl.BlockSpec((None, None, tm, H_DIFF, D_HEAD), lambda i: (0, i // per_seq, i % per_seq, 0, 0))
    shapes = [
        jax.ShapeDtypeStruct((n, d_grp), ret_dtype),
        jax.ShapeDtypeStruct((n, d_grp), ret_dtype),
        jax.ShapeDtypeStruct((n, d_grp), ret_dtype),
        jax.ShapeDtypeStruct((n, d_grp), F32),
        jax.ShapeDtypeStruct((n, d_grp), ret_dtype),
        kv_shape,
        kv_shape,
        jax.ShapeDtypeStruct((n, d_grp), F32),
        jax.ShapeDtypeStruct((n, d_grp), BF16),
        jax.ShapeDtypeStruct((d_grp, n), BF16),
    ]
    out_specs = [out_blk] * 5 + [kv_blk, kv_blk, out_blk, out_blk, pl.BlockSpec((d_grp, tm), lambda i: (0, i))]
    return pl.pallas_call(
        _proj_kernel,
        grid=(n // tm,),
        in_specs=[
            pl.BlockSpec((tm, d_model), tok),
            pl.BlockSpec((1, d_model), lambda i: (0, 0)),
            pl.BlockSpec((d_model, d_in), lambda i: (0, 0)),
            pl.BlockSpec((tm, D_HEAD), lambda i: (i % n_tab, 0)),
            pl.BlockSpec((tm, D_HEAD), lambda i: (i % n_tab, 0)),
        ],
        out_specs=out_specs,
        out_shape=shapes,
        compiler_params=_cparams(("parallel",)),
        name="proj",
    )(x, norm_pre.reshape(1, d_model), w_bf16, cos2, sin2)


def _rope_tables(pos):
    half = D_HEAD // 2
    inv = ROPE_BASE ** (-np.arange(half, dtype=np.float64) / half)
    ang = np.asarray(pos, np.float64)[:, None] * inv[None, :]
    cos, sin = np.cos(ang).astype(np.float32), np.sin(ang).astype(np.float32)
    return jnp.asarray(np.concatenate([cos, cos], axis=-1)), jnp.asarray(np.concatenate([-sin, sin], axis=-1))


def _group_norm_gate(o, gain, gate):
    mu = jnp.mean(o, axis=-1, keepdims=True)
    d = o - mu
    var = jnp.mean(d * d, axis=-1, keepdims=True)
    return d * lax.rsqrt(var + GN_EPS) * gain * _silu(gate)


def _decay_terms(n_rows, chunk, lg):
    row = lax.broadcasted_iota(jnp.int32, (n_rows, n_rows), 0)
    col = lax.broadcasted_iota(jnp.int32, (n_rows, n_rows), 1)
    dist = (row - col).astype(F32)
    dmask = jnp.where(col <= row, jnp.exp(jnp.maximum(dist, 0.0) * lg), 0.0)
    if n_rows != chunk:
        dmask = jnp.where(row // chunk == col // chunk, dmask, 0.0)
    idx = (lax.broadcasted_iota(jnp.int32, (n_rows, 1), 0) % chunk).astype(F32)
    xi = jnp.exp((idx + 1.0) * lg)
    zeta = jnp.exp((chunk - 1.0 - idx) * lg)
    g_chunk = jnp.exp(jnp.full((1, D_HEAD), chunk, F32) * lg)
    return dmask, xi, zeta, g_chunk


def _ret_prompt_kernel(lg_ref, q_ref, k_ref, v_ref, gate_ref, gn_ref, y_ref, s_out_ref,
                       s_scr, dmask_scr, xi_scr, zeta_scr, gch_scr, *, n_ck):
    c = pl.program_id(1)

    @pl.when(c == 0)
    def _():
        s_scr[...] = jnp.zeros_like(s_scr)
        for h in range(H_RET):
            dmask, xi, zeta, g_chunk = _decay_terms(RET_CHUNK, RET_CHUNK, lg_ref[h])
            dmask_scr[h] = dmask
            xi_scr[h] = jnp.broadcast_to(xi, (RET_CHUNK, D_HEAD))
            zeta_scr[h] = jnp.broadcast_to(zeta, (RET_CHUNK, D_HEAD))
            gch_scr[h] = jnp.broadcast_to(g_chunk, (8, D_HEAD))

    for h in range(H_RET):
        sl = slice(h * D_HEAD, (h + 1) * D_HEAD)
        s = s_scr[h]
        for a in range(n_ck):
            rows = slice(a * RET_CHUNK, (a + 1) * RET_CHUNK)
            q = q_ref[rows, sl]
            k = k_ref[rows, sl]
            v = v_ref[rows, sl]
            inner = _nt(q, k) * dmask_scr[h]
            o = jnp.dot(inner.astype(BF16), v, preferred_element_type=F32)
            o = o + jnp.dot(q, s.astype(BF16), preferred_element_type=F32) * xi_scr[h]
            kz = (k.astype(F32) * zeta_scr[h]).astype(BF16)
            s = s * gch_scr[h, 0:1, :] + _tn(kz, v)
            y_ref[rows, sl] = _group_norm_gate(o, gn_ref[h:h + 1, :], gate_ref[rows, sl]).astype(y_ref.dtype)
        s_scr[h] = s

    @pl.when(c == pl.num_programs(1) - 1)
    def _():
        s_out_ref[...] = s_scr[...]


def _ret_prompt(lg, qr, kr, vr, gr, gn, batch, seq, n_ck=4):
    n_chunk = seq // (RET_CHUNK * n_ck)
    d_grp = H_RET * D_HEAD
    blk = pl.BlockSpec((RET_CHUNK * n_ck, d_grp), lambda b, c: (b * n_chunk + c, 0))
    tile = pltpu.VMEM((H_RET, RET_CHUNK, D_HEAD), F32)
    return pl.pallas_call(
        functools.partial(_ret_prompt_kernel, n_ck=n_ck),
        grid=(batch, n_chunk),
        in_specs=[
            pl.BlockSpec(memory_space=pltpu.SMEM),
            blk, blk, blk, blk,
            pl.BlockSpec((H_RET, D_HEAD), lambda b, c: (0, 0)),
        ],
        out_specs=[
            blk,
            pl.BlockSpec((None, H_RET, D_HEAD, D_HEAD), lambda b, c: (b, 0, 0, 0)),
        ],
        out_shape=[
            jax.ShapeDtypeStruct((batch * seq, d_grp), BF16),
            jax.ShapeDtypeStruct((batch, H_RET, D_HEAD, D_HEAD), F32),
        ],
        scratch_shapes=[pltpu.VMEM((H_RET, D_HEAD, D_HEAD), F32), tile, tile, tile,
                        pltpu.VMEM((H_RET, 8, D_HEAD), F32)],
        compiler_params=_cparams(("parallel", "arbitrary")),
        name="ret_prompt",
    )(lg, qr, kr, vr, gr, gn)


def _ret_sample_kernel(lg_ref, q_ref, k_ref, v_ref, gate_ref, gn_ref, s_ref, y_ref, s_out_ref, *, n_seq, t):
    n_rows = n_seq * t
    seq_of_row = lax.broadcasted_iota(jnp.int32, (n_rows, 1), 0) // t
    for h in range(H_RET):
        sl = slice(h * D_HEAD, (h + 1) * D_HEAD)
        lg = lg_ref[h]
        dmask, xi, zeta, g_chunk = _decay_terms(n_rows, t, lg)
        qf = q_ref[:, sl]
        q = qf.astype(BF16)
        v = v_ref[:, sl].astype(BF16)
        kf = k_ref[:, sl]
        inner = _nt(q, kf.astype(BF16)) * dmask
        o = jnp.dot(inner.astype(BF16), v, preferred_element_type=F32)
        kz = kf * zeta
        cross = []
        for b in range(n_seq):
            s_old = s_ref[b, h]
            q_b = qf[b * t:(b + 1) * t, :].astype(BF16)
            cross.append(jnp.dot(q_b, s_old.astype(BF16), preferred_element_type=F32))
            kz_b = jnp.where(seq_of_row == b, kz, 0.0).astype(BF16)
            s_out_ref[b, h] = s_old * g_chunk + _tn(kz_b, v)
        o = o + jnp.concatenate(cross, axis=0) * xi
        y_ref[:, sl] = _group_norm_gate(o, gn_ref[h:h + 1, :], gate_ref[:, sl]).astype(y_ref.dtype)


def _ret_sample(lg, qr, kr, vr, gr, gn, state, n_batch, t, n_seq):
    d_grp = H_RET * D_HEAD
    blk = pl.BlockSpec((n_seq * t, d_grp), lambda i: (i, 0))
    sblk = pl.BlockSpec((n_seq, H_RET, D_HEAD, D_HEAD), lambda i: (i, 0, 0, 0))
    return pl.pallas_call(
        functools.partial(_ret_sample_kernel, n_seq=n_seq, t=t),
        grid=(n_batch // n_seq,),
        in_specs=[
            pl.BlockSpec(memory_space=pltpu.SMEM),
            blk, blk, blk, blk,
            pl.BlockSpec((H_RET, D_HEAD), lambda i: (0, 0)),
            sblk,
        ],
        out_specs=[blk, sblk],
        out_shape=[
            jax.ShapeDtypeStruct((n_batch * t, d_grp), BF16),
            jax.ShapeDtypeStruct((n_batch, H_RET, D_HEAD, D_HEAD), F32),
        ],
        compiler_params=_cparams(("parallel",)),
        name="ret_sample",
    )(lg, qr, kr, vr, gr, gn, state)


def _lambda(lq1_ref, lk1_ref, lq2_ref, lk2_ref, lam_init):
    a = jnp.sum(lq1_ref[...] * lk1_ref[...], axis=-1, keepdims=True)
    b = jnp.sum(lq2_ref[...] * lk2_ref[...], axis=-1, keepdims=True)
    return jnp.exp(a) - jnp.exp(b) + lam_init


def _diff_finish(o, gain, gate, lam_init):
    y = o * lax.rsqrt(jnp.mean(o * o, axis=-1, keepdims=True) + NORM_EPS)
    return y * gain * (1.0 - lam_init) * _silu(gate)


def _map_split(q):
    lane = lax.broadcasted_iota(jnp.int32, q.shape, 1)
    zero = jnp.zeros_like(q)
    return jnp.concatenate([jnp.where(lane < DK_DIFF, q, zero), jnp.where(lane >= DK_DIFF, q, zero)], axis=0)


def _diff_prompt_tile(h, qi, lam, slope_ref, q_ref, k_ref, vt_ref, gate_ref, gain_ref, y_ref,
                      m_ref, l_ref, acc_ref, qbd_ref, kaux_ref, sa_ref, sb_ref, *, tq, tk, cw, lam_init):
    slope = slope_ref[h]
    lane_q = lax.broadcasted_iota(jnp.int32, (2 * tq, D_HEAD), 1)
    qbd_ref[:, :D_HEAD] = _map_split(q_ref[...])
    qbd_ref[:, D_HEAD:] = jnp.where(lane_q < 2, 1.0, 0.0).astype(BF16)
    row_k = lax.broadcasted_iota(jnp.int32, (tk, D_HEAD), 0)
    lane_k = lax.broadcasted_iota(jnp.int32, (tk, D_HEAD), 1)
    j_part = jnp.where(lane_k == 0, (row_k // 16) * 16, jnp.where(lane_k == 1, row_k % 16, 0))
    kaux_ref[...] = (j_part.astype(F32) * slope).astype(BF16)

    m_ref[...] = jnp.full_like(m_ref, NEG_INF)
    l_ref[...] = jnp.zeros_like(l_ref)
    acc_ref[...] = jnp.zeros_like(acc_ref)

    def scores(ki, s_ref):
        start = pl.multiple_of(ki * tk, tk)
        k_aug = jnp.concatenate([k_ref[pl.ds(start, tk), :], kaux_ref[...]], axis=1)
        s_ref[...] = _nt(k_aug, qbd_ref[...])

    def absorb(ki, s_ref, diag_off):
        start = pl.multiple_of(ki * tk, tk)
        vt = vt_ref[:, pl.ds(start, tk)]
        if diag_off is not None:
            kml = lax.broadcasted_iota(jnp.int32, (tk, cw), 0) - lax.broadcasted_iota(jnp.int32, (tk, cw), 1)
        for c in range(2 * tq // cw):
            cols = slice(c * cw, (c + 1) * cw)
            q0 = (c * cw) % tq
            if diag_off is not None and diag_off >= q0 + cw:
                continue
            s = s_ref[:, cols]
            if diag_off is not None:
                s = jnp.where(kml <= q0 - diag_off, s, NEG_INF)
            off = slope * (qi * tq - ki * tk).astype(F32)
            m = m_ref[:, cols]
            m_new = jnp.maximum(m, jnp.max(s, axis=0, keepdims=True) - off)
            corr = jnp.exp(m - m_new)
            p = jnp.exp(s - (m_new + off))
            l_ref[:, cols] = l_ref[:, cols] * corr + jnp.sum(p, axis=0, keepdims=True)
            acc_ref[:, cols] = acc_ref[:, cols] * corr + jnp.dot(vt, p.astype(BF16), preferred_element_type=F32)
            m_ref[:, cols] = m_new

    assert tq == 2 * tk
    n_full = 2 * qi
    scores(0, sa_ref)

    def body(pi, carry):
        scores(2 * pi + 1, sb_ref)
        absorb(2 * pi, sa_ref, None)
        scores(2 * pi + 2, sa_ref)
        absorb(2 * pi + 1, sb_ref, None)
        return carry

    def run_pairs(lo, hi):
        lax.fori_loop(lo, hi, body, 0)

    def finish():
        scores(n_full + 1, sb_ref)
        absorb(n_full, sa_ref, 0)
        absorb(n_full + 1, sb_ref, tk)
        o_t = acc_ref[...] / l_ref[...]
        o = (o_t[:, :tq] - lam * o_t[:, tq:]).T
        y_ref[...] = _diff_finish(o, gain_ref[pl.ds(h, 1), :], gate_ref[...], lam_init).astype(y_ref.dtype)

    return run_pairs, finish


def _diff_sample_sequence(seq, lam, pt_ref, q_ref, kl_ref, vl_ref, gate_ref, gain_ref, ck_hbm, cv_hbm, y_ref,
                          qpair_ref, kpad_ref, vpad_ref, bias_ref, m_ref, l_ref, acc_ref, kbuf, vbuf, sem,
                          *, t, n_seq, n_pp, n_grp, n_slots, n_steps, past, lam_init):
    n_total = n_seq * n_steps
    rows_h = 2 * t
    n_rows = H_DIFF * rows_h
    ppg = n_pp // n_grp

    def batch_copies(src_step, slot, known_pages):
        for i in range(n_pp):
            page = pt_ref[src_step * n_pp + i] if known_pages else 0
            yield pltpu.make_async_copy(ck_hbm.at[page], kbuf.at[slot, i], sem.at[slot])
            yield pltpu.make_async_copy(cv_hbm.at[page], vbuf.at[slot, i], sem.at[slot])

    def start_batch(src_step):
        for cp in batch_copies(src_step, src_step % n_slots, True):
            cp.start()

    @pl.when(seq == 0)
    def _():
        for d in range(n_slots - 1):
            start_batch(d)

    def alibi(width, causal):
        rowi = lax.broadcasted_iota(jnp.int32, (n_rows, width), 0)
        tok_k = lax.broadcasted_iota(jnp.int32, (n_rows, width), 1)
        slope = jnp.exp2(-8.0 * ((rowi // rows_h).astype(F32) + 1.0) / H_DIFF)
        tok_q = rowi % t
        bias = -slope * (tok_q - tok_k).astype(F32)
        if causal:
            bias = jnp.where(tok_k <= tok_q, bias, NEG_INF)
        return bias

    def update(g, k_heads, v_heads, bias, shift):
        pairs = range(0, H_DIFF, 2)
        sa = jnp.concatenate(
            [_nt(qpair_ref[h // 2], jnp.concatenate([k_heads[h], k_heads[h + 1]], axis=1)) for h in pairs],
            axis=0) + bias
        m = m_ref[g]
        m_new = jnp.maximum(m, jnp.max(sa, axis=-1, keepdims=True) - shift)
        corr = jnp.exp(m - m_new)
        p = jnp.exp(sa - (m_new + shift))
        l_ref[g] = l_ref[g] * corr + jnp.sum(p, axis=-1, keepdims=True)
        pv = []
        for h in pairs:
            both = jnp.dot(p[h * rows_h:(h + 2) * rows_h, :].astype(BF16),
                           jnp.concatenate([v_heads[h], v_heads[h + 1]], axis=1), preferred_element_type=F32)
            pv += [both[:rows_h, :D_HEAD], both[rows_h:, D_HEAD:]]
        acc_ref[g] = acc_ref[g] * corr + jnp.concatenate(pv, axis=0)
        m_ref[g] = m_new

    rowi = lax.broadcasted_iota(jnp.int32, (n_rows, 1), 0)
    slope = jnp.exp2(-8.0 * ((rowi // rows_h).astype(F32) + 1.0) / H_DIFF)

    @pl.when(seq == 0)
    def _():
        bias_ref[:, :PAGE] = alibi(PAGE, causal=True) + slope * float(past)
        bias_ref[:, PAGE:] = alibi(ppg * PAGE, causal=False)
        kpad_ref[...] = jnp.zeros_like(kpad_ref)
        vpad_ref[...] = jnp.zeros_like(vpad_ref)

    for h in range(0, H_DIFF, 2):
        q0 = _map_split(q_ref[:, h * D_HEAD:(h + 1) * D_HEAD])
        q1 = _map_split(q_ref[:, (h + 1) * D_HEAD:(h + 2) * D_HEAD])
        zero = jnp.zeros_like(q0)
        qpair_ref[h // 2] = jnp.concatenate(
            [jnp.concatenate([q0, zero], axis=1), jnp.concatenate([zero, q1], axis=1)], axis=0).astype(BF16)
    kpad_ref[0:t, :] = kl_ref[...]
    vpad_ref[0:t, :] = vl_ref[...]
    m_ref[...] = jnp.full_like(m_ref, NEG_INF)
    l_ref[...] = jnp.zeros_like(l_ref)
    acc_ref[...] = jnp.zeros_like(acc_ref)

    def head_rows(ref, h):
        return ref[pl.ds(h, PAGE, stride=H_DIFF), :].astype(BF16)

    def new_token_rows(ref, h):
        return ref[:, h * D_HEAD:(h + 1) * D_HEAD].astype(BF16)

    def run_step(j):
        step = seq * n_steps + j

        @pl.when(step + (n_slots - 1) < n_total)
        def _():
            start_batch(step + (n_slots - 1))

        slot = step % n_slots
        for cp in batch_copies(step, slot, False):
            cp.wait()
        for g in range(n_grp):
            pages = range(g * ppg, (g + 1) * ppg)
            first_key = (j * n_pp + g * ppg) * PAGE
            shift = slope * float(past - first_key)
            first = j == 0 and g == 0
            k_heads = [jnp.concatenate(([new_token_rows(kpad_ref, h)] if first else [])
                                       + [head_rows(kbuf.at[slot, i], h) for i in pages], axis=0)
                       for h in range(H_DIFF)]
            v_heads = [jnp.concatenate(([new_token_rows(vpad_ref, h)] if first else [])
                                       + [head_rows(vbuf.at[slot, i], h) for i in pages], axis=0)
                       for h in range(H_DIFF)]
            update(g, k_heads, v_heads, bias_ref[...] if first else bias_ref[:, PAGE:], shift)

    def finish():
        m_all = m_ref[0]
        for g in range(1, n_grp):
            m_all = jnp.maximum(m_all, m_ref[g])
        l_all = jnp.zeros_like(m_all)
        acc = jnp.zeros((n_rows, D_HEAD), F32)
        for g in range(n_grp):
            w = jnp.exp(m_ref[g] - m_all)
            l_all = l_all + l_ref[g] * w
            acc = acc + acc_ref[g] * w
        o_all = acc / l_all
        for h in range(H_DIFF):
            sl = slice(h * D_HEAD, (h + 1) * D_HEAD)
            o = o_all[h * rows_h:h * rows_h + t, :] - lam * o_all[h * rows_h + t:(h + 1) * rows_h, :]
            y_ref[:, sl] = _diff_finish(o, gain_ref[h:h + 1, :], gate_ref[:, sl], lam_init).astype(y_ref.dtype)

    return run_step, finish


def _diff_attention_kernel(pt_ref, slope_ref, qp_ref, kp_ref, vtp_ref, gatep_ref,
                           qs_ref, kls_ref, vls_ref, gates_ref, gain_ref,
                           lq1_ref, lk1_ref, lq2_ref, lk2_ref, ck_hbm, cv_hbm, yp_ref, ys_ref,
                           m_ref, l_ref, acc_ref, qbd_ref, kaux_ref, sa_ref, sb_ref,
                           qpair_ref, kpad_ref, vpad_ref, bias_ref, ms_ref, ls_ref, accs_ref, kbuf, vbuf, sem,
                           *, nq, prompt_kw, sample_kw):
    g = pl.program_id(0)
    lam = _lambda(lq1_ref, lk1_ref, lq2_ref, lk2_ref, prompt_kw["lam_init"])
    page_step, finish_sequence = _diff_sample_sequence(
        g, lam, pt_ref, qs_ref, kls_ref, vls_ref, gates_ref, gain_ref, ck_hbm, cv_hbm, ys_ref,
        qpair_ref, kpad_ref, vpad_ref, bias_ref, ms_ref, ls_ref, accs_ref, kbuf, vbuf, sem, **sample_kw)
    h = (g // nq) % H_DIFF
    qi = _tile_order(g % nq, nq)
    key_pairs, finish_tile = _diff_prompt_tile(
        h, qi, lam, slope_ref, qp_ref, kp_ref, vtp_ref, gatep_ref, gain_ref, yp_ref,
        m_ref, l_ref, acc_ref, qbd_ref, kaux_ref, sa_ref, sb_ref, **prompt_kw)
    n_steps = sample_kw["n_steps"]
    for j in range(n_steps // 2):
        page_step(j)
    key_pairs(0, qi // 2)
    for j in range(n_steps // 2, n_steps):
        page_step(j)
    key_pairs(qi // 2, qi)
    finish_tile()
    finish_sequence()


def _tile_order(i, nq):
    return jnp.where(i % 2 == 0, nq - 1 - i // 2, i // 2)


def _diff_attention(slopes, qd, kd16, vdt16, gd, qd_s, kd_s, vd_s, gd_s, gain, lams, page_table, cache_k, cache_v,
                    batch, seq, n_dec, t, lam_init, tq=512, tk=256, cw=256, n_pp=16, n_grp=1, n_slots=6):
    assert tq % cw == 0
    nq = seq // tq
    n_pages = page_table.shape[1]
    past = n_pages * PAGE
    d_grp = H_DIFF * D_HEAD
    n_rows = H_DIFF * 2 * t
    n_steps = n_pages // n_pp
    assert n_dec == batch * H_DIFF * nq and n_pages % n_pp == 0 and n_dec * n_steps >= n_slots - 1
    pt_flat = page_table.reshape(-1)
    page_cols = PAGE * H_DIFF
    n_pool = cache_k.shape[1]
    cache_k = cache_k.reshape(n_pool, page_cols, D_HEAD)
    cache_v = cache_v.reshape(n_pool, page_cols, D_HEAD)

    def tile(g):
        return g // (H_DIFF * nq), (g // nq) % H_DIFF, _tile_order(g % nq, nq)

    def q_map(g, pt):
        b, h, qi = tile(g)
        return b * nq + qi, h

    qblk = pl.BlockSpec((tq, D_HEAD), q_map)
    tokblk = pl.BlockSpec((t, d_grp), lambda g, pt: (g, 0))
    vec = pl.BlockSpec((1, DK_DIFF), lambda g, pt: (0, 0))
    hbm = pl.BlockSpec(memory_space=pl.ANY)
    grid_spec = pltpu.PrefetchScalarGridSpec(
        num_scalar_prefetch=1,
        grid=(n_dec,),
        in_specs=[
            pl.BlockSpec(memory_space=pltpu.SMEM),
            qblk,
            pl.BlockSpec((seq, D_HEAD), lambda g, pt: (tile(g)[0], tile(g)[1])),
            pl.BlockSpec((D_HEAD, seq), lambda g, pt: (tile(g)[1], tile(g)[0])),
            qblk,
            tokblk, tokblk, tokblk, tokblk,
            pl.BlockSpec((H_DIFF, D_HEAD), lambda g, pt: (0, 0)),
            vec, vec, vec, vec, hbm, hbm,
        ],
        out_specs=[qblk, tokblk],
        scratch_shapes=[
            pltpu.VMEM((1, 2 * tq), F32),
            pltpu.VMEM((1, 2 * tq), F32),
            pltpu.VMEM((D_HEAD, 2 * tq), F32),
            pltpu.VMEM((2 * tq, 2 * D_HEAD), BF16),
            pltpu.VMEM((tk, D_HEAD), BF16),
            pltpu.VMEM((tk, 2 * tq), F32),
            pltpu.VMEM((tk, 2 * tq), F32),
            pltpu.VMEM((H_DIFF // 2, n_rows // 2, 2 * D_HEAD), BF16),
            pltpu.VMEM((PAGE, d_grp), F32),
            pltpu.VMEM((PAGE, d_grp), F32),
            pltpu.VMEM((n_rows, (1 + n_pp // n_grp) * PAGE), F32),
            pltpu.VMEM((n_grp, n_rows, 1), F32),
            pltpu.VMEM((n_grp, n_rows, 1), F32),
            pltpu.VMEM((n_grp, n_rows, D_HEAD), F32),
            pltpu.VMEM((n_slots, n_pp, page_cols, D_HEAD), F32),
            pltpu.VMEM((n_slots, n_pp, page_cols, D_HEAD), F32),
            pltpu.SemaphoreType.DMA((n_slots,)),
        ],
    )
    return pl.pallas_call(
        functools.partial(
            _diff_attention_kernel, nq=nq,
            prompt_kw=dict(tq=tq, tk=tk, cw=cw, lam_init=lam_init),
            sample_kw=dict(t=t, n_seq=n_dec, n_pp=n_pp, n_grp=n_grp, n_slots=n_slots, n_steps=n_steps, past=past,
                           lam_init=lam_init)),
        grid_spec=grid_spec,
        out_shape=[jax.ShapeDtypeStruct((batch * seq, d_grp), BF16),
                   jax.ShapeDtypeStruct((n_dec * t, d_grp), F32)],
        compiler_params=_cparams(("arbitrary",)),
        name="diff_attention",
    )(pt_flat, slopes, qd, kd16, vdt16, gd, qd_s, kd_s, vd_s, gd_s, gain, *lams, cache_k, cache_v)


def _finish_kernel(x_ref, yr_ref, yd_ref, w_ref, g_ref, o_ref):
    d_grp = yr_ref.shape[1]
    y = jnp.dot(yr_ref[...].astype(BF16), w_ref[:d_grp, :], preferred_element_type=F32)
    y = y + jnp.dot(yd_ref[...].astype(BF16), w_ref[d_grp:, :], preferred_element_type=F32)
    yn = y * lax.rsqrt(jnp.mean(y * y, axis=-1, keepdims=True) + NORM_EPS) * g_ref[...]
    o_ref[...] = x_ref[...] + yn


def _finish(x, y_ret, y_diff, w_bf16, norm_post, tm):
    n, d_model = x.shape
    d_grp = y_ret.shape[1]
    tok = lambda i: (i, 0)
    return pl.pallas_call(
        _finish_kernel,
        grid=(n // tm,),
        in_specs=[
            pl.BlockSpec((tm, d_model), tok),
            pl.BlockSpec((tm, d_grp), tok),
            pl.BlockSpec((tm, d_grp), tok),
            pl.BlockSpec((2 * d_grp, d_model), lambda i: (0, 0)),
            pl.BlockSpec((1, d_model), lambda i: (0, 0)),
        ],
        out_specs=pl.BlockSpec((tm, d_model), tok),
        out_shape=jax.ShapeDtypeStruct((n, d_model), F32),
        compiler_params=_cparams(("parallel",)),
        name="finish",
    )(x, y_ret, y_diff, w_bf16, norm_post.reshape(1, d_model))


def kernel(x_prompt, x_sample, cache_k, cache_v, page_table, state_ret, norm_pre, norm_post, w_in,
           ret_gn, diff_lq1, diff_lk1, diff_lq2, diff_lk2, diff_norm, w_out):
    batch, seq, d_model = x_prompt.shape
    n_dec, t_dec, _ = x_sample.shape
    depth = w_in.shape[0]
    assert depth == 1
    n_pages = page_table.shape[1]
    past = n_pages * PAGE
    layer = 0
    lam_init = 0.8 - 0.6 * math.exp(-0.3 * layer)

    lg = jnp.log(1.0 - jnp.exp2(-5.0 - jnp.arange(H_RET, dtype=F32)))
    assert 8 % H_DIFF == 0
    slope_exponents = -(8 // H_DIFF) * (jnp.arange(H_DIFF, dtype=jnp.int32) + 1)
    slopes = lax.bitcast_convert_type((slope_exponents + 127) << 23, F32)
    w_in16 = w_in[layer].astype(BF16)
    w_out16 = w_out[layer].astype(BF16)
    lams = [p[layer].reshape(1, DK_DIFF) for p in (diff_lq1, diff_lk1, diff_lq2, diff_lk2)]
    tm = 512

    xp = x_prompt.reshape(batch * seq, d_model)
    cos_p, sin_p = _rope_tables(np.arange(seq))
    qr, kr, vr, gr, qd, kd, vd, gd, kd16, vdt16 = _project(
        xp, norm_pre[layer], w_in16, cos_p, sin_p, tm, BF16, batch_seq=(batch, seq))
    xs = x_sample.reshape(n_dec * t_dec, d_model)
    cos_s, sin_s = _rope_tables(past + np.arange(tm) % t_dec)
    qr_s, kr_s, vr_s, gr_s, qd_s, kd_s, vd_s, gd_s, _, _ = _project(xs, norm_pre[layer], w_in16, cos_s, sin_s, tm, F32)

    y_ret, s_fin = _ret_prompt(lg, qr, kr, vr, gr, ret_gn[layer], batch, seq)
    y_ret_s, s_new = _ret_sample(lg, qr_s, kr_s, vr_s, gr_s, ret_gn[layer], state_ret[layer], n_dec, t_dec, n_seq=16)
    y_diff, y_diff_s = _diff_attention(slopes, qd, kd16, vdt16, gd, qd_s, kd_s, vd_s, gd_s, diff_norm[layer], lams,
                                       page_table, cache_k, cache_v, batch, seq, n_dec, t_dec, lam_init)
    out_p = _finish(xp, y_ret, y_diff, w_out16, norm_post[layer], tm)
    out_s = _finish(xs, y_ret_s, y_diff_s, w_out16, norm_post[layer], tm)

    return (
        out_p.reshape(batch, seq, d_model),
        out_s.reshape(n_dec, t_dec, d_model),
        kd,
        vd,
        s_fin.reshape(1, batch, H_RET, D_HEAD, D_HEAD),
        kd_s.reshape(1, n_dec, t_dec, H_DIFF, D_HEAD),
        vd_s.reshape(1, n_dec, t_dec, H_DIFF, D_HEAD),
        s_new.reshape(1, n_dec, H_RET, D_HEAD, D_HEAD),
    )
```

```python
import functools
import math

import jax
import jax.numpy as jnp
import numpy as np
from jax import lax
from jax.experimental import pallas as pl
from jax.experimental.pallas import tpu as pltpu

F32 = jnp.float32
BF16 = jnp.bfloat16

H_RET = 4
H_DIFF = 4
D_HEAD = 128
DK_DIFF = 64
RET_CHUNK = 128
PAGE = 128
ROPE_BASE = 10000.0
NORM_EPS = 1e-6
GN_EPS = 1e-5
NEG_INF = float("-inf")

VMEM_LIMIT = 56 * 1024 * 1024


def _cparams(sem):
    return pltpu.CompilerParams(dimension_semantics=sem, vmem_limit_bytes=VMEM_LIMIT)


def _silu(x):
    return x * (1.0 / (1.0 + jnp.exp(-x)))


def _nt(a, b):
    return lax.dot_general(a, b, (((1,), (1,)), ((), ())), preferred_element_type=F32)


def _tn(a, b):
    return lax.dot_general(a, b, (((0,), (0,)), ((), ())), preferred_element_type=F32)


def _proj_sections(x_ref, g_ref, w_ref):
    d_grp = H_RET * D_HEAD
    x = x_ref[...]
    ms = jnp.mean(x * x, axis=-1, keepdims=True)
    h = (x * lax.rsqrt(ms + NORM_EPS) * g_ref[...]).astype(BF16)

    def section(s):
        return jnp.dot(h, w_ref[:, s * d_grp:(s + 1) * d_grp], preferred_element_type=F32)

    return section


def _rope_heads(z, cos2, sin2, scale):
    out = []
    for hd in range(H_RET):
        zh = z[:, hd * D_HEAD:(hd + 1) * D_HEAD]
        rot = pltpu.roll(zh, D_HEAD // 2, axis=1)
        out.append((zh * cos2 + rot * sin2) * scale)
    return out


def _proj_diff_streams(section, qd_ref, kd_ref, vd_ref, gd_ref, kd16_ref, vdt16_ref):
    def store_heads(ref, z):
        if len(ref.shape) == 2:
            ref[...] = z
        else:
            for hd in range(H_DIFF):
                ref[:, hd, :] = z[:, hd * D_HEAD:(hd + 1) * D_HEAD]

    qd_ref[...] = (section(4) * (DK_DIFF ** -0.5)).astype(qd_ref.dtype)
    z = section(5)
    store_heads(kd_ref, z)
    kd16_ref[...] = z.astype(BF16)
    z = section(6)
    store_heads(vd_ref, z)
    vdt16_ref[...] = z.T.astype(BF16)
    gd_ref[...] = section(7)


def _proj_kernel(x_ref, g_ref, w_ref, cos_ref, sin_ref,
                 qr_ref, kr_ref, vr_ref, gr_ref, qd_ref, kd_ref, vd_ref, gd_ref, kd16_ref, vdt16_ref):
    section = _proj_sections(x_ref, g_ref, w_ref)
    cos2 = cos_ref[...]
    sin2 = sin_ref[...]
    for hd, r in enumerate(_rope_heads(section(0), cos2, sin2, 1.0)):
        qr_ref[:, hd * D_HEAD:(hd + 1) * D_HEAD] = r.astype(qr_ref.dtype)
    for hd, r in enumerate(_rope_heads(section(1), cos2, sin2, D_HEAD ** -0.5)):
        kr_ref[:, hd * D_HEAD:(hd + 1) * D_HEAD] = r.astype(kr_ref.dtype)
    vr_ref[...] = section(2).astype(vr_ref.dtype)
    gr_ref[...] = section(3)
    _proj_diff_streams(section, qd_ref, kd_ref, vd_ref, gd_ref, kd16_ref, vdt16_ref)


def _project(x, norm_pre, w_bf16, cos2, sin2, tm, ret_dtype, batch_seq=None):
    n, d_model = x.shape
    d_in = w_bf16.shape[1]
    d_grp = d_in // 8
    n_tab = cos2.shape[0] // tm
    tok = lambda i: (i, 0)
    out_blk = pl.BlockSpec((tm, d_grp), tok)
    if batch_seq is None:
        kv_shape = jax.ShapeDtypeStruct((n, d_grp), F32)
        kv_blk = out_blk
    else:
        batch, seq = batch_seq
        per_seq = seq // tm
        kv_shape = jax.ShapeDtypeStruct((1, batch, seq, H_DIFF, D_HEAD), F32)
        kv_blk = pl.BlockSpec((None, None, tm, H_DIFF, D_HEAD), lambda i: (0, i // per_seq, i % per_seq, 0, 0))
    shapes = [
        jax.ShapeDtypeStruct((n, d_grp), ret_dtype),
        jax.ShapeDtypeStruct((n, d_grp), ret_dtype),
        jax.ShapeDtypeStruct((n, d_grp), ret_dtype),
        jax.ShapeDtypeStruct((n, d_grp), F32),
        jax.ShapeDtypeStruct((n, d_grp), ret_dtype),
        kv_shape,
        kv_shape,
        jax.ShapeDtypeStruct((n, d_grp), F32),
        jax.ShapeDtypeStruct((n, d_grp), BF16),
        jax.ShapeDtypeStruct((d_grp, n), BF16),
    ]
    out_specs = [out_blk] * 5 + [kv_blk, kv_blk, out_blk, out_blk, pl.BlockSpec((d_grp, tm), lambda i: (0, i))]
    return pl.pallas_call(
        _proj_kernel,
        grid=(n // tm,),
        in_specs=[
            pl.BlockSpec((tm, d_model), tok),
            pl.BlockSpec((1, d_model), lambda i: (0, 0)),
            pl.BlockSpec((d_model, d_in), lambda i: (0, 0)),
            pl.BlockSpec((tm, D_HEAD), lambda i: (i % n_tab, 0)),
            pl.BlockSpec((tm, D_HEAD), lambda i: (i % n_tab, 0)),
        ],
        out_specs=out_specs,
        out_shape=shapes,
        compiler_params=_cparams(("parallel",)),
        name="proj",
    )(x, norm_pre.reshape(1, d_model), w_bf16, cos2, sin2)


def _rope_tables(pos):
    half = D_HEAD // 2
    inv = ROPE_BASE ** (-np.arange(half, dtype=np.float64) / half)
    ang = np.asarray(pos, np.float64)[:, None] * inv[None, :]
    cos, sin = np.cos(ang).astype(np.float32), np.sin(ang).astype(np.float32)
    return jnp.asarray(np.concatenate([cos, cos], axis=-1)), jnp.asarray(np.concatenate([-sin, sin], axis=-1))


def _group_norm_gate(o, gain, gate):
    mu = jnp.mean(o, axis=-1, keepdims=True)
    d = o - mu
    var = jnp.mean(d * d, axis=-1, keepdims=True)
    return d * lax.rsqrt(var + GN_EPS) * gain * _silu(gate)


def _decay_terms(n_rows, chunk, lg):
    row = lax.broadcasted_iota(jnp.int32, (n_rows, n_rows), 0)
    col = lax.broadcasted_iota(jnp.int32, (n_rows, n_rows), 1)
    dist = (row - col).astype(F32)
    dmask = jnp.where(col <= row, jnp.exp(jnp.maximum(dist, 0.0) * lg), 0.0)
    if n_rows != chunk:
        dmask = jnp.where(row // chunk == col // chunk, dmask, 0.0)
    idx = (lax.broadcasted_iota(jnp.int32, (n_rows, 1), 0) % chunk).astype(F32)
    xi = jnp.exp((idx + 1.0) * lg)
    zeta = jnp.exp((chunk - 1.0 - idx) * lg)
    g_chunk = jnp.exp(jnp.full((1, D_HEAD), chunk, F32) * lg)
    return dmask, xi, zeta, g_chunk


def _ret_prompt_kernel(lg_ref, q_ref, k_ref, v_ref, gate_ref, gn_ref, y_ref, s_out_ref,
                       s_scr, dmask_scr, xi_scr, zeta_scr, gch_scr, *, n_ck):
    c = pl.program_id(1)

    @pl.when(c == 0)
    def _():
        s_scr[...] = jnp.zeros_like(s_scr)
        for h in range(H_RET):
            dmask, xi, zeta, g_chunk = _decay_terms(RET_CHUNK, RET_CHUNK, lg_ref[h])
            dmask_scr[h] = dmask
            xi_scr[h] = jnp.broadcast_to(xi, (RET_CHUNK, D_HEAD))
            zeta_scr[h] = jnp.broadcast_to(zeta, (RET_CHUNK, D_HEAD))
            gch_scr[h] = jnp.broadcast_to(g_chunk, (8, D_HEAD))

    for h in range(H_RET):
        sl = slice(h * D_HEAD, (h + 1) * D_HEAD)
        s = s_scr[h]
        for a in range(n_ck):
            rows = slice(a * RET_CHUNK, (a + 1) * RET_CHUNK)
            q = q_ref[rows, sl]
            k = k_ref[rows, sl]
            v = v_ref[rows, sl]
            inner = _nt(q, k) * dmask_scr[h]
            o = jnp.dot(inner.astype(BF16), v, preferred_element_type=F32)
            o = o + jnp.dot(q, s.astype(BF16), preferred_element_type=F32) * xi_scr[h]
            kz = (k.astype(F32) * zeta_scr[h]).astype(BF16)
            s = s * gch_scr[h, 0:1, :] + _tn(kz, v)
            y_ref[rows, sl] = _group_norm_gate(o, gn_ref[h:h + 1, :], gate_ref[rows, sl]).astype(y_ref.dtype)
        s_scr[h] = s

    @pl.when(c == pl.num_programs(1) - 1)
    def _():
        s_out_ref[...] = s_scr[...]


def _ret_prompt(lg, qr, kr, vr, gr, gn, batch, seq, n_ck=8):
    n_chunk = seq // (RET_CHUNK * n_ck)
    d_grp = H_RET * D_HEAD
    blk = pl.BlockSpec((RET_CHUNK * n_ck, d_grp), lambda b, c: (b * n_chunk + c, 0))
    tile = pltpu.VMEM((H_RET, RET_CHUNK, D_HEAD), F32)
    return pl.pallas_call(
        functools.partial(_ret_prompt_kernel, n_ck=n_ck),
        grid=(batch, n_chunk),
        in_specs=[
            pl.BlockSpec(memory_space=pltpu.SMEM),
            blk, blk, blk, blk,
            pl.BlockSpec((H_RET, D_HEAD), lambda b, c: (0, 0)),
        ],
        out_specs=[
            blk,
            pl.BlockSpec((None, H_RET, D_HEAD, D_HEAD), lambda b, c: (b, 0, 0, 0)),
        ],
        out_shape=[
            jax.ShapeDtypeStruct((batch * seq, d_grp), BF16),
            jax.ShapeDtypeStruct((batch, H_RET, D_HEAD, D_HEAD), F32),
        ],
        scratch_shapes=[pltpu.VMEM((H_RET, D_HEAD, D_HEAD), F32), tile, tile, tile,
                        pltpu.VMEM((H_RET, 8, D_HEAD), F32)],
        compiler_params=_cparams(("parallel", "arbitrary")),
        name="ret_prompt",
    )(lg, qr, kr, vr, gr, gn)


def _ret_sample_kernel(lg_ref, q_ref, k_ref, v_ref, gate_ref, gn_ref, s_ref, y_ref, s_out_ref, *, n_seq, t):
    n_rows = n_seq * t
    seq_of_row = lax.broadcasted_iota(jnp.int32, (n_rows, 1), 0) // t
    for h in range(H_RET):
        sl = slice(h * D_HEAD, (h + 1) * D_HEAD)
        lg = lg_ref[h]
        dmask, xi, zeta, g_chunk = _decay_terms(n_rows, t, lg)
        qf = q_ref[:, sl]
        q = qf.astype(BF16)
        v = v_ref[:, sl].astype(BF16)
        kf = k_ref[:, sl]
        inner = _nt(q, kf.astype(BF16)) * dmask
        o = jnp.dot(inner.astype(BF16), v, preferred_element_type=F32)
        kz = kf * zeta
        cross = []
        for b in range(n_seq):
            s_old = s_ref[b, h]
            q_b = qf[b * t:(b + 1) * t, :].astype(BF16)
            cross.append(jnp.dot(q_b, s_old.astype(BF16), preferred_element_type=F32))
            kz_b = jnp.where(seq_of_row == b, kz, 0.0).astype(BF16)
            s_out_ref[b, h] = s_old * g_chunk + _tn(kz_b, v)
        o = o + jnp.concatenate(cross, axis=0) * xi
        y_ref[:, sl] = _group_norm_gate(o, gn_ref[h:h + 1, :], gate_ref[:, sl]).astype(y_ref.dtype)


def _ret_sample(lg, qr, kr, vr, gr, gn, state, n_batch, t, n_seq):
    d_grp = H_RET * D_HEAD
    blk = pl.BlockSpec((n_seq * t, d_grp), lambda i: (i, 0))
    sblk = pl.BlockSpec((n_seq, H_RET, D_HEAD, D_HEAD), lambda i: (i, 0, 0, 0))
    return pl.pallas_call(
        functools.partial(_ret_sample_kernel, n_seq=n_seq, t=t),
        grid=(n_batch // n_seq,),
        in_specs=[
            pl.BlockSpec(memory_space=pltpu.SMEM),
            blk, blk, blk, blk,
            pl.BlockSpec((H_RET, D_HEAD), lambda i: (0, 0)),
            sblk,
        ],
        out_specs=[blk, sblk],
        out_shape=[
            jax.ShapeDtypeStruct((n_batch * t, d_grp), BF16),
            jax.ShapeDtypeStruct((n_batch, H_RET, D_HEAD, D_HEAD), F32),
        ],
        compiler_params=_cparams(("parallel",)),
        name="ret_sample",
    )(lg, qr, kr, vr, gr, gn, state)


def _lambda(lq1_ref, lk1_ref, lq2_ref, lk2_ref, lam_init):
    a = jnp.sum(lq1_ref[...] * lk1_ref[...], axis=-1, keepdims=True)
    b = jnp.sum(lq2_ref[...] * lk2_ref[...], axis=-1, keepdims=True)
    return jnp.exp(a) - jnp.exp(b) + lam_init


def _diff_finish(o, gain, gate, lam_init):
    y = o * lax.rsqrt(jnp.mean(o * o, axis=-1, keepdims=True) + NORM_EPS)
    return y * gain * (1.0 - lam_init) * _silu(gate)


def _map_split(q):
    lane = lax.broadcasted_iota(jnp.int32, q.shape, 1)
    zero = jnp.zeros_like(q)
    return jnp.concatenate([jnp.where(lane < DK_DIFF, q, zero), jnp.where(lane >= DK_DIFF, q, zero)], axis=0)


def _diff_prompt_tile(h, qi, lam, slope_ref, q_ref, k_ref, vt_ref, gate_ref, gain_ref, y_ref,
                      m_ref, l_ref, acc_ref, qbd_ref, kaux_ref, sa_ref, sb_ref, *, tq, tk, cw, lam_init):
    slope = slope_ref[h]
    lane_q = lax.broadcasted_iota(jnp.int32, (2 * tq, D_HEAD), 1)
    qbd_ref[:, :D_HEAD] = _map_split(q_ref[...])
    qbd_ref[:, D_HEAD:] = jnp.where(lane_q < 2, 1.0, 0.0).astype(BF16)
    row_k = lax.broadcasted_iota(jnp.int32, (tk, D_HEAD), 0)
    lane_k = lax.broadcasted_iota(jnp.int32, (tk, D_HEAD), 1)
    j_part = jnp.where(lane_k == 0, (row_k // 16) * 16, jnp.where(lane_k == 1, row_k % 16, 0))
    kaux_ref[...] = (j_part.astype(F32) * slope).astype(BF16)

    m_ref[...] = jnp.full_like(m_ref, NEG_INF)
    l_ref[...] = jnp.zeros_like(l_ref)
    acc_ref[...] = jnp.zeros_like(acc_ref)

    def scores(ki, s_ref):
        start = pl.multiple_of(ki * tk, tk)
        k_aug = jnp.concatenate([k_ref[pl.ds(start, tk), :], kaux_ref[...]], axis=1)
        s_ref[...] = _nt(k_aug, qbd_ref[...])

    def absorb(ki, s_ref, diag_off):
        start = pl.multiple_of(ki * tk, tk)
        vt = vt_ref[:, pl.ds(start, tk)]
        if diag_off is not None:
            kml = lax.broadcasted_iota(jnp.int32, (tk, cw), 0) - lax.broadcasted_iota(jnp.int32, (tk, cw), 1)
        for c in range(2 * tq // cw):
            cols = slice(c * cw, (c + 1) * cw)
            q0 = (c * cw) % tq
            if diag_off is not None and diag_off >= q0 + cw:
                continue
            s = s_ref[:, cols]
            if diag_off is not None:
                s = jnp.where(kml <= q0 - diag_off, s, NEG_INF)
            off = slope * (qi * tq - ki * tk).astype(F32)
            m = m_ref[:, cols]
            m_new = jnp.maximum(m, jnp.max(s, axis=0, keepdims=True) - off)
            corr = jnp.exp(m - m_new)
            p = jnp.exp(s - (m_new + off))
            l_ref[:, cols] = l_ref[:, cols] * corr + jnp.sum(p, axis=0, keepdims=True)
            acc_ref[:, cols] = acc_ref[:, cols] * corr + jnp.dot(vt, p.astype(BF16), preferred_element_type=F32)
            m_ref[:, cols] = m_new

    assert tq == 2 * tk
    n_full = 2 * qi
    scores(0, sa_ref)

    def body(pi, carry):
        scores(2 * pi + 1, sb_ref)
        absorb(2 * pi, sa_ref, None)
        scores(2 * pi + 2, sa_ref)
        absorb(2 * pi + 1, sb_ref, None)
        return carry

    def run_pairs(lo, hi):
        lax.fori_loop(lo, hi, body, 0)

    def finish():
        scores(n_full + 1, sb_ref)
        absorb(n_full, sa_ref, 0)
        absorb(n_full + 1, sb_ref, tk)
        o_t = acc_ref[...] / l_ref[...]
        o = (o_t[:, :tq] - lam * o_t[:, tq:]).T
        y_ref[...] = _diff_finish(o, gain_ref[pl.ds(h, 1), :], gate_ref[...], lam_init).astype(y_ref.dtype)

    return run_pairs, finish


def _diff_sample_sequence(seq, lam, pt_ref, q_ref, kl_ref, vl_ref, gate_ref, gain_ref, ck_hbm, cv_hbm, y_ref,
                          qpair_ref, kpad_ref, vpad_ref, bias_ref, m_ref, l_ref, acc_ref, kbuf, vbuf, sem,
                          *, t, n_seq, n_pp, n_grp, n_slots, n_steps, past, lam_init):
    n_total = n_seq * n_steps
    rows_h = 2 * t
    n_rows = H_DIFF * rows_h
    ppg = n_pp // n_grp

    def batch_copies(src_step, slot, known_pages):
        for i in range(n_pp):
            page = pt_ref[src_step * n_pp + i] if known_pages else 0
            yield pltpu.make_async_copy(ck_hbm.at[page], kbuf.at[slot, i], sem.at[slot])
            yield pltpu.make_async_copy(cv_hbm.at[page], vbuf.at[slot, i], sem.at[slot])

    def start_batch(src_step):
        for cp in batch_copies(src_step, src_step % n_slots, True):
            cp.start()

    @pl.when(seq == 0)
    def _():
        for d in range(n_slots - 1):
            start_batch(d)

    def alibi(width, causal):
        rowi = lax.broadcasted_iota(jnp.int32, (n_rows, width), 0)
        tok_k = lax.broadcasted_iota(jnp.int32, (n_rows, width), 1)
        slope = jnp.exp2(-8.0 * ((rowi // rows_h).astype(F32) + 1.0) / H_DIFF)
        tok_q = rowi % t
        bias = -slope * (tok_q - tok_k).astype(F32)
        if causal:
            bias = jnp.where(tok_k <= tok_q, bias, NEG_INF)
        return bias

    def update(g, k_heads, v_heads, bias, shift):
        pairs = range(0, H_DIFF, 2)
        sa = jnp.concatenate(
            [_nt(qpair_ref[h // 2], jnp.concatenate([k_heads[h], k_heads[h + 1]], axis=1)) for h in pairs],
            axis=0) + bias
        m = m_ref[g]
        m_new = jnp.maximum(m, jnp.max(sa, axis=-1, keepdims=True) - shift)
        corr = jnp.exp(m - m_new)
        p = jnp.exp(sa - (m_new + shift))
        l_ref[g] = l_ref[g] * corr + jnp.sum(p, axis=-1, keepdims=True)
        pv = []
        for h in pairs:
            both = jnp.dot(p[h * rows_h:(h + 2) * rows_h, :].astype(BF16),
                           jnp.concatenate([v_heads[h], v_heads[h + 1]], axis=1), preferred_element_type=F32)
            pv += [both[:rows_h, :D_HEAD], both[rows_h:, D_HEAD:]]
        acc_ref[g] = acc_ref[g] * corr + jnp.concatenate(pv, axis=0)
        m_ref[g] = m_new

    rowi = lax.broadcasted_iota(jnp.int32, (n_rows, 1), 0)
    slope = jnp.exp2(-8.0 * ((rowi // rows_h).astype(F32) + 1.0) / H_DIFF)

    @pl.when(seq == 0)
    def _():
        bias_ref[:, :PAGE] = alibi(PAGE, causal=True) + slope * float(past)
        bias_ref[:, PAGE:] = alibi(ppg * PAGE, causal=False)
        kpad_ref[...] = jnp.zeros_like(kpad_ref)
        vpad_ref[...] = jnp.zeros_like(vpad_ref)

    for h in range(0, H_DIFF, 2):
        q0 = _map_split(q_ref[:, h * D_HEAD:(h + 1) * D_HEAD])
        q1 = _map_split(q_ref[:, (h + 1) * D_HEAD:(h + 2) * D_HEAD])
        zero = jnp.zeros_like(q0)
        qpair_ref[h // 2] = jnp.concatenate(
            [jnp.concatenate([q0, zero], axis=1), jnp.concatenate([zero, q1], axis=1)], axis=0).astype(BF16)
    kpad_ref[0:t, :] = kl_ref[...]
    vpad_ref[0:t, :] = vl_ref[...]
    m_ref[...] = jnp.full_like(m_ref, NEG_INF)
    l_ref[...] = jnp.zeros_like(l_ref)
    acc_ref[...] = jnp.zeros_like(acc_ref)

    def head_rows(ref, h):
        return ref[pl.ds(h, PAGE, stride=H_DIFF), :].astype(BF16)

    def new_token_rows(ref, h):
        return ref[:, h * D_HEAD:(h + 1) * D_HEAD].astype(BF16)

    def run_step(j):
        step = seq * n_steps + j

        @pl.when(step + (n_slots - 1) < n_total)
        def _():
            start_batch(step + (n_slots - 1))

        slot = step % n_slots
        for cp in batch_copies(step, slot, False):
            cp.wait()
        for g in range(n_grp):
            pages = range(g * ppg, (g + 1) * ppg)
            first_key = (j * n_pp + g * ppg) * PAGE
            shift = slope * float(past - first_key)
            first = j == 0 and g == 0
            k_heads = [jnp.concatenate(([new_token_rows(kpad_ref, h)] if first else [])
                                       + [head_rows(kbuf.at[slot, i], h) for i in pages], axis=0)
                       for h in range(H_DIFF)]
            v_heads = [jnp.concatenate(([new_token_rows(vpad_ref, h)] if first else [])
                                       + [head_rows(vbuf.at[slot, i], h) for i in pages], axis=0)
                       for h in range(H_DIFF)]
            update(g, k_heads, v_heads, bias_ref[...] if first else bias_ref[:, PAGE:], shift)

    def finish():
        m_all = m_ref[0]
        for g in range(1, n_grp):
            m_all = jnp.maximum(m_all, m_ref[g])
        l_all = jnp.zeros_like(m_all)
        acc = jnp.zeros((n_rows, D_HEAD), F32)
        for g in range(n_grp):
            w = jnp.exp(m_ref[g] - m_all)
            l_all = l_all + l_ref[g] * w
            acc = acc + acc_ref[g] * w
        o_all = acc / l_all
        for h in range(H_DIFF):
            sl = slice(h * D_HEAD, (h + 1) * D_HEAD)
            o = o_all[h * rows_h:h * rows_h + t, :] - lam * o_all[h * rows_h + t:(h + 1) * rows_h, :]
            y_ref[:, sl] = _diff_finish(o, gain_ref[h:h + 1, :], gate_ref[:, sl], lam_init).astype(y_ref.dtype)

    return run_step, finish


def _diff_attention_kernel(pt_ref, slope_ref, qp_ref, kp_ref, vtp_ref, gatep_ref,
                           qs_ref, kls_ref, vls_ref, gates_ref, gain_ref,
                           lq1_ref, lk1_ref, lq2_ref, lk2_ref, ck_hbm, cv_hbm, yp_ref, ys_ref,
                           m_ref, l_ref, acc_ref, qbd_ref, kaux_ref, sa_ref, sb_ref,
                           qpair_ref, kpad_ref, vpad_ref, bias_ref, ms_ref, ls_ref, accs_ref, kbuf, vbuf, sem,
                           *, nq, prompt_kw, sample_kw):
    g = pl.program_id(0)
    lam = _lambda(lq1_ref, lk1_ref, lq2_ref, lk2_ref, prompt_kw["lam_init"])
    page_step, finish_sequence = _diff_sample_sequence(
        g, lam, pt_ref, qs_ref, kls_ref, vls_ref, gates_ref, gain_ref, ck_hbm, cv_hbm, ys_ref,
        qpair_ref, kpad_ref, vpad_ref, bias_ref, ms_ref, ls_ref, accs_ref, kbuf, vbuf, sem, **sample_kw)
    h = (g // nq) % H_DIFF
    qi = _tile_order(g % nq, nq)
    key_pairs, finish_tile = _diff_prompt_tile(
        h, qi, lam, slope_ref, qp_ref, kp_ref, vtp_ref, gatep_ref, gain_ref, yp_ref,
        m_ref, l_ref, acc_ref, qbd_ref, kaux_ref, sa_ref, sb_ref, **prompt_kw)
    n_steps = sample_kw["n_steps"]
    for j in range(n_steps // 2):
        page_step(j)
    key_pairs(0, qi // 2)
    for j in range(n_steps // 2, n_steps):
        page_step(j)
    key_pairs(qi // 2, qi)
    finish_tile()
    finish_sequence()


def _tile_order(i, nq):
    return jnp.where(i % 2 == 0, nq - 1 - i // 2, i // 2)


def _diff_attention(slopes, qd, kd16, vdt16, gd, qd_s, kd_s, vd_s, gd_s, gain, lams, page_table, cache_k, cache_v,
                    batch, seq, n_dec, t, lam_init, tq=512, tk=256, cw=256, n_pp=16, n_grp=1, n_slots=5):
    assert tq % cw == 0
    nq = seq // tq
    n_pages = page_table.shape[1]
    past = n_pages * PAGE
    d_grp = H_DIFF * D_HEAD
    n_rows = H_DIFF * 2 * t
    n_steps = n_pages // n_pp
    assert n_dec == batch * H_DIFF * nq and n_pages % n_pp == 0 and n_dec * n_steps >= n_slots - 1
    pt_flat = page_table.reshape(-1)
    page_cols = PAGE * H_DIFF
    n_pool = cache_k.shape[1]
    cache_k = cache_k.reshape(n_pool, page_cols, D_HEAD)
    cache_v = cache_v.reshape(n_pool, page_cols, D_HEAD)

    def tile(g):
        return g // (H_DIFF * nq), (g // nq) % H_DIFF, _tile_order(g % nq, nq)

    def q_map(g, pt):
        b, h, qi = tile(g)
        return b * nq + qi, h

    qblk = pl.BlockSpec((tq, D_HEAD), q_map)
    tokblk = pl.BlockSpec((t, d_grp), lambda g, pt: (g, 0))
    vec = pl.BlockSpec((1, DK_DIFF), lambda g, pt: (0, 0))
    hbm = pl.BlockSpec(memory_space=pl.ANY)
    grid_spec = pltpu.PrefetchScalarGridSpec(
        num_scalar_prefetch=1,
        grid=(n_dec,),
        in_specs=[
            pl.BlockSpec(memory_space=pltpu.SMEM),
            qblk,
            pl.BlockSpec((seq, D_HEAD), lambda g, pt: (tile(g)[0], tile(g)[1])),
            pl.BlockSpec((D_HEAD, seq), lambda g, pt: (tile(g)[1], tile(g)[0])),
            qblk,
            tokblk, tokblk, tokblk, tokblk,
            pl.BlockSpec((H_DIFF, D_HEAD), lambda g, pt: (0, 0)),
            vec, vec, vec, vec, hbm, hbm,
        ],
        out_specs=[qblk, tokblk],
        scratch_shapes=[
            pltpu.VMEM((1, 2 * tq), F32),
            pltpu.VMEM((1, 2 * tq), F32),
            pltpu.VMEM((D_HEAD, 2 * tq), F32),
            pltpu.VMEM((2 * tq, 2 * D_HEAD), BF16),
            pltpu.VMEM((tk, D_HEAD), BF16),
            pltpu.VMEM((tk, 2 * tq), F32),
            pltpu.VMEM((tk, 2 * tq), F32),
            pltpu.VMEM((H_DIFF // 2, n_rows // 2, 2 * D_HEAD), BF16),
            pltpu.VMEM((PAGE, d_grp), F32),
            pltpu.VMEM((PAGE, d_grp), F32),
            pltpu.VMEM((n_rows, (1 + n_pp // n_grp) * PAGE), F32),
            pltpu.VMEM((n_grp, n_rows, 1), F32),
            pltpu.VMEM((n_grp, n_rows, 1), F32),
            pltpu.VMEM((n_grp, n_rows, D_HEAD), F32),
            pltpu.VMEM((n_slots, n_pp, page_cols, D_HEAD), F32),
            pltpu.VMEM((n_slots, n_pp, page_cols, D_HEAD), F32),
            pltpu.SemaphoreType.DMA((n_slots,)),
        ],
    )
    return pl.pallas_call(
        functools.partial(
            _diff_attention_kernel, nq=nq,
            prompt_kw=dict(tq=tq, tk=tk, cw=cw, lam_init=lam_init),
            sample_kw=dict(t=t, n_seq=n_dec, n_pp=n_pp, n_grp=n_grp, n_slots=n_slots, n_steps=n_steps, past=past,
                           lam_init=lam_init)),
        grid_spec=grid_spec,
        out_shape=[jax.ShapeDtypeStruct((batch * seq, d_grp), BF16),
                   jax.ShapeDtypeStruct((n_dec * t, d_grp), F32)],
        compiler_params=_cparams(("arbitrary",)),
        name="diff_attention",
    )(pt_flat, slopes, qd, kd16, vdt16, gd, qd_s, kd_s, vd_s, gd_s, gain, *lams, cache_k, cache_v)


def _finish_kernel(x_ref, yr_ref, yd_ref, w_ref, g_ref, o_ref):
    d_grp = yr_ref.shape[1]
    y = jnp.dot(yr_ref[...].astype(BF16), w_ref[:d_grp, :], preferred_element_type=F32)
    y = y + jnp.dot(yd_ref[...].astype(BF16), w_ref[d_grp:, :], preferred_element_type=F32)
    yn = y * lax.rsqrt(jnp.mean(y * y, axis=-1, keepdims=True) + NORM_EPS) * g_ref[...]
    o_ref[...] = x_ref[...] + yn


def _finish(x, y_ret, y_diff, w_bf16, norm_post, tm):
    n, d_model = x.shape
    d_grp = y_ret.shape[1]
    tok = lambda i: (i, 0)
    return pl.pallas_call(
        _finish_kernel,
        grid=(n // tm,),
        in_specs=[
            pl.BlockSpec((tm, d_model), tok),
            pl.BlockSpec((tm, d_grp), tok),
            pl.BlockSpec((tm, d_grp), tok),
            pl.BlockSpec((2 * d_grp, d_model), lambda i: (0, 0)),
            pl.BlockSpec((1, d_model), lambda i: (0, 0)),
        ],
        out_specs=pl.BlockSpec((tm, d_model), tok),
        out_shape=jax.ShapeDtypeStruct((n, d_model), F32),
        compiler_params=_cparams(("parallel",)),
        name="finish",
    )(x, y_ret, y_diff, w_bf16, norm_post.reshape(1, d_model))


def kernel(x_prompt, x_sample, cache_k, cache_v, page_table, state_ret, norm_pre, norm_post, w_in,
           ret_gn, diff_lq1, diff_lk1, diff_lq2, diff_lk2, diff_norm, w_out):
    batch, seq, d_model = x_prompt.shape
    n_dec, t_dec, _ = x_sample.shape
    depth = w_in.shape[0]
    assert depth == 1
    n_pages = page_table.shape[1]
    past = n_pages * PAGE
    layer = 0
    lam_init = 0.8 - 0.6 * math.exp(-0.3 * layer)

    lg = jnp.log(1.0 - jnp.exp2(-5.0 - jnp.arange(H_RET, dtype=F32)))
    assert 8 % H_DIFF == 0
    slope_exponents = -(8 // H_DIFF) * (jnp.arange(H_DIFF, dtype=jnp.int32) + 1)
    slopes = lax.bitcast_convert_type((slope_exponents + 127) << 23, F32)
    w_in16 = w_in[layer].astype(BF16)
    w_out16 = w_out[layer].astype(BF16)
    lams = [p[layer].reshape(1, DK_DIFF) for p in (diff_lq1, diff_lk1, diff_lq2, diff_lk2)]
    tm = 512

    xp = x_prompt.reshape(batch * seq, d_model)
    cos_p, sin_p = _rope_tables(np.arange(seq))
    qr, kr, vr, gr, qd, kd, vd, gd, kd16, vdt16 = _project(
        xp, norm_pre[layer], w_in16, cos_p, sin_p, tm, BF16, batch_seq=(batch, seq))
    xs = x_sample.reshape(n_dec * t_dec, d_model)
    cos_s, sin_s = _rope_tables(past + np.arange(tm) % t_dec)
    qr_s, kr_s, vr_s, gr_s, qd_s, kd_s, vd_s, gd_s, _, _ = _project(xs, norm_pre[layer], w_in16, cos_s, sin_s, tm, F32)

    y_ret, s_fin = _ret_prompt(lg, qr, kr, vr, gr, ret_gn[layer], batch, seq)
    y_ret_s, s_new = _ret_sample(lg, qr_s, kr_s, vr_s, gr_s, ret_gn[layer], state_ret[layer], n_dec, t_dec, n_seq=16)
    y_diff, y_diff_s = _diff_attention(slopes, qd, kd16, vdt16, gd, qd_s, kd_s, vd_s, gd_s, diff_norm[layer], lams,
                                       page_table, cache_k, cache_v, batch, seq, n_dec, t_dec, lam_init)
    out_p = _finish(xp, y_ret, y_diff, w_out16, norm_post[layer], 2 * tm)
    out_s = _finish(xs, y_ret_s, y_diff_s, w_out16, norm_post[layer], tm)

    return (
        out_p.reshape(batch, seq, d_model),
        out_s.reshape(n_dec, t_dec, d_model),
        kd,
        vd,
        s_fin.reshape(1, batch, H_RET, D_HEAD, D_HEAD),
        kd_s.reshape(1, n_dec, t_dec, H_DIFF, D_HEAD),
        vd_s.reshape(1, n_dec, t_dec, H_DIFF, D_HEAD),
        s_new.reshape(1, n_dec, H_RET, D_HEAD, D_HEAD),
    )
```
